```python
import math
import jax
import jax.numpy as jnp
from jax import lax
import numpy as np

D_MODEL = 2048
BATCH = 4
SEQ = 2048
DEPTH = 4
DEC_BATCH = 8
DEC_SEQ = 1
PAST_LEN = 16384
PAGE_SIZE = 128

N_MIXERS = 2
N_ATT = (DEPTH + N_MIXERS - 1) // N_MIXERS
N_RG = DEPTH // N_MIXERS
N_HEADS = 16
HEAD_DIM = D_MODEL // N_HEADS
MOBA_BLOCK = 256
MOBA_TOP_K = 3
Q_CHUNK = 16
D_RNN = D_MODEL
RG_BLOCKS = 8
RG_BLOCK_W = D_RNN // RG_BLOCKS
RG_C = 8.0
RG_CONV_W = 4
D_FF = 5632
FFN_CONV_W = 3
NORM_EPS = 1e-6

kernel_name = 'moba_rglru_convffn_hybrid_step'


def _rmsnorm(x, g):
    x32 = x.astype(jnp.float32)
    y = x32 * lax.rsqrt(jnp.mean(x32 * x32, axis=-1, keepdims=True) + NORM_EPS)
    return (y * g.astype(jnp.float32)).astype(x.dtype)


def _alibi_slopes():
    h = jnp.arange(1, N_HEADS + 1, dtype=jnp.float32)
    return jnp.exp2(-8.0 * h / N_HEADS)


def _causal_dwconv(x, buf, w, b):
    s = x.shape[1]
    width = w.shape[0]
    xp = jnp.concatenate([buf.astype(x.dtype), x], axis=1)
    y = xp[:, 0:s] * w[0]
    for k in range(1, width):
        y = y + xp[:, k:k + s] * w[k]
    return y + b, xp[:, s:]


def _gather_blocks(blocks, idx):
    return blocks[idx]


def _gather_pages(cache, page_table):
    nseq, npg = page_table.shape
    rows = cache[page_table]
    return rows.reshape(nseq, npg * cache.shape[1], cache.shape[2], cache.shape[3])


def _moba_attention(q, k_all, v_all, q_pos):
    b, sq, h, hd = q.shape
    l = k_all.shape[1]
    nb = -(-l // MOBA_BLOCK)
    pad = nb * MOBA_BLOCK - l
    kb = jnp.pad(k_all, ((0, 0), (0, pad), (0, 0), (0, 0))).reshape(b, nb, MOBA_BLOCK, h, hd)
    vb = jnp.pad(v_all, ((0, 0), (0, pad), (0, 0), (0, 0))).reshape(b, nb, MOBA_BLOCK, h, hd)
    k_mean = jnp.mean(kb.astype(jnp.float32), axis=2)
    own = q_pos // MOBA_BLOCK
    gate = jnp.einsum('bqhd,bnhd->bhqn', q.astype(jnp.float32), k_mean)
    fully_past = jnp.arange(nb)[None, :] < own[:, None]
    gate = jnp.where(fully_past[None, None], gate, -jnp.inf)
    n_top = min(MOBA_TOP_K, nb)
    _, top_idx = lax.top_k(gate, n_top)
    top_valid = jnp.arange(n_top)[None, :] < jnp.minimum(own, MOBA_TOP_K)[:, None]
    top_idx = jnp.where(top_valid[None, None], top_idx, own[None, None, :, None])
    blk_idx = jnp.concatenate(
        [top_idx, jnp.broadcast_to(own[None, None, :, None], (b, h, sq, 1))], axis=-1).astype(jnp.int32)
    blk_valid = jnp.concatenate([top_valid, jnp.ones((sq, 1), dtype=bool)], axis=-1)
    n_sel = n_top + 1
    qc = math.gcd(Q_CHUNK, sq)
    nc = sq // qc
    slopes = _alibi_slopes()
    scale = 1.0 / math.sqrt(hd)
    gather = jax.vmap(jax.vmap(_gather_blocks, in_axes=(2, 0)), in_axes=(0, 0))

    def chunk_fn(args):
        q_c, idx, pos, valid = args
        kg = gather(kb, idx)
        vg = gather(vb, idx)
        kpos = idx[..., None] * MOBA_BLOCK + jnp.arange(MOBA_BLOCK)
        logits = jnp.einsum('bhqd,bhqjsd->bhqjs', q_c, kg).astype(jnp.float32) * scale
        dist = (pos[:, None, None] - kpos).astype(jnp.float32)
        logits = logits - slopes[None, :, None, None, None] * dist
        mask = valid[:, :, None] & (kpos <= pos[:, None, None])
        logits = jnp.where(mask, logits, -jnp.inf)
        p = jax.nn.softmax(logits.reshape(b, h, qc, n_sel * MOBA_BLOCK), axis=-1).reshape(logits.shape)
        return jnp.einsum('bhqjs,bhqjsd->bhqd', p.astype(vg.dtype), vg)

    xs = (q.reshape(b, nc, qc, h, hd).transpose(1, 0, 3, 2, 4),
          blk_idx.reshape(b, h, nc, qc, n_sel).transpose(2, 0, 1, 3, 4),
          q_pos.reshape(nc, qc),
          blk_valid.reshape(nc, qc, n_sel))
    out = lax.map(chunk_fn, xs)
    return out.transpose(1, 0, 3, 2, 4).reshape(b, sq, h, hd)


def _moba_mixer(x, w_qkv, w_o, k_past, v_past):
    b, s, _ = x.shape
    q, k, v = jnp.split(x @ w_qkv, 3, axis=-1)
    q = q.reshape(b, s, N_HEADS, HEAD_DIM)
    k = k.reshape(b, s, N_HEADS, HEAD_DIM)
    v = v.reshape(b, s, N_HEADS, HEAD_DIM)
    if k_past is None:
        k_all, v_all, start = k, v, 0
    else:
        k_all = jnp.concatenate([k_past.astype(k.dtype), k], axis=1)
        v_all = jnp.concatenate([v_past.astype(v.dtype), v], axis=1)
        start = k_past.shape[1]
    q_pos = start + jnp.arange(s, dtype=jnp.int32)
    o = _moba_attention(q, k_all, v_all, q_pos)
    return o.reshape(b, s, N_HEADS * HEAD_DIM) @ w_o, k, v


def _block_diag(x, w, bias):
    lead = x.shape[:-1]
    y = jnp.einsum('...nc,ncd->...nd', x.reshape(*lead, RG_BLOCKS, RG_BLOCK_W), w)
    return y.reshape(*lead, D_RNN) + bias


def _linear_scan(a, bx, h0):
    bx = bx.at[:, 0].add(a[:, 0] * h0)

    def combine(l, r):
        return (l[0] * r[0], r[0] * l[1] + r[1])

    _, h = lax.associative_scan(combine, (a, bx), axis=1)
    return h


def _rglru_mixer(x, h0, conv_buf, w_gate, w_x, conv_w, conv_b, w_a, b_a, w_i, b_i, lam, w_out):
    y_gate = jax.nn.gelu(x @ w_gate, approximate=True)
    xc, new_buf = _causal_dwconv(x @ w_x, conv_buf, conv_w, conv_b)
    r = jax.nn.sigmoid(_block_diag(xc, w_a, b_a).astype(jnp.float32))
    i = jax.nn.sigmoid(_block_diag(xc, w_i, b_i).astype(jnp.float32))
    log_a = -RG_C * r * jax.nn.softplus(-lam.astype(jnp.float32))
    a = jnp.exp(log_a)
    mult = jnp.sqrt(-jnp.expm1(2.0 * log_a))
    h = _linear_scan(a, mult * (i * xc.astype(jnp.float32)), h0.astype(jnp.float32))
    out = (h.astype(x.dtype) * y_gate) @ w_out
    return out, h[:, -1].astype(h0.dtype), new_buf


def _conv_ffn(x, buf, w_gate, w_up, conv_w, conv_b, w_down):
    g, new_buf = _causal_dwconv(x @ w_gate, buf, conv_w, conv_b)
    h = jax.nn.gelu(g, approximate=False) * (x @ w_up)
    return h @ w_down, new_buf


def setup_inputs(seed: int = 0) -> dict:
    key = jax.random.key(seed)
    keys = jax.random.split(key, 32)

    def nrm(i, shape, scale):
        return jax.random.normal(keys[i], shape, jnp.float32) * scale

    n_pages = PAST_LEN // PAGE_SIZE
    n_used = DEC_BATCH * n_pages
    n_pool = (5 * n_used + 3) // 4
    page_table = jax.random.permutation(keys[6], n_pool)[:n_used].reshape(DEC_BATCH, n_pages).astype(jnp.int32)
    cshape = (n_pool, PAGE_SIZE, N_HEADS, HEAD_DIM)
    hd_all = N_HEADS * HEAD_DIM
    u = jax.random.uniform(keys[23], (N_RG, D_RNN), jnp.float32, 0.9, 0.999)
    s = u ** (1.0 / RG_C)
    rg_lambda = jnp.log(s) - jnp.log1p(-s)
    return {
        'x_prompt': nrm(0, (BATCH, SEQ, D_MODEL), 1.0),
        'x_sample': nrm(1, (DEC_BATCH, DEC_SEQ, D_MODEL), 1.0),
        'cache_k_l0': nrm(2, cshape, 1.0),
        'cache_v_l0': nrm(3, cshape, 1.0),
        'cache_k_l2': nrm(4, cshape, 1.0),
        'cache_v_l2': nrm(5, cshape, 1.0),
        'page_table': page_table,
        'state_rglru_h': nrm(7, (N_RG, DEC_BATCH, D_RNN), 0.5),
        'state_rglru_conv': nrm(8, (N_RG, DEC_BATCH, RG_CONV_W - 1, D_RNN), 1.0),
        'state_ffn_conv': nrm(9, (DEPTH, DEC_BATCH, FFN_CONV_W - 1, D_FF), 1.0),
        'norm_mix': 1.0 + nrm(10, (DEPTH, D_MODEL), 0.01),
        'norm_ffn': 1.0 + nrm(11, (DEPTH, D_MODEL), 0.01),
        'norm_final': 1.0 + nrm(12, (D_MODEL,), 0.01),
        'attn_w_qkv': nrm(13, (N_ATT, D_MODEL, 3 * hd_all), D_MODEL ** -0.5),
        'attn_w_o': nrm(14, (N_ATT, hd_all, D_MODEL), hd_all ** -0.5),
        'rg_w_gate': nrm(15, (N_RG, D_MODEL, D_RNN), D_MODEL ** -0.5),
        'rg_w_x': nrm(16, (N_RG, D_MODEL, D_RNN), D_MODEL ** -0.5),
        'rg_conv_w': nrm(17, (N_RG, RG_CONV_W, D_RNN), RG_CONV_W ** -0.5),
        'rg_conv_b': nrm(18, (N_RG, D_RNN), 0.01),
        'rg_w_a': nrm(19, (N_RG, RG_BLOCKS, RG_BLOCK_W, RG_BLOCK_W), RG_BLOCK_W ** -0.5),
        'rg_b_a': nrm(20, (N_RG, D_RNN), 0.1),
        'rg_w_i': nrm(21, (N_RG, RG_BLOCKS, RG_BLOCK_W, RG_BLOCK_W), RG_BLOCK_W ** -0.5),
        'rg_b_i': nrm(22, (N_RG, D_RNN), 0.1),
        'rg_lambda': rg_lambda,
        'rg_w_out': nrm(24, (N_RG, D_RNN, D_MODEL), D_RNN ** -0.5),
        'ffn_w_gate': nrm(25, (DEPTH, D_MODEL, D_FF), D_MODEL ** -0.5),
        'ffn_w_up': nrm(26, (DEPTH, D_MODEL, D_FF), D_MODEL ** -0.5),
        'ffn_conv_w': nrm(27, (DEPTH, FFN_CONV_W, D_FF), FFN_CONV_W ** -0.5),
        'ffn_conv_b': nrm(28, (DEPTH, D_FF), 0.01),
        'ffn_w_down': nrm(29, (DEPTH, D_FF, D_MODEL), D_FF ** -0.5),
    }


def reference(x_prompt, x_sample, cache_k_l0, cache_v_l0, cache_k_l2, cache_v_l2, page_table,
              state_rglru_h, state_rglru_conv, state_ffn_conv, norm_mix, norm_ffn, norm_final,
              attn_w_qkv, attn_w_o, rg_w_gate, rg_w_x, rg_conv_w, rg_conv_b, rg_w_a, rg_b_a,
              rg_w_i, rg_b_i, rg_lambda, rg_w_out, ffn_w_gate, ffn_w_up, ffn_conv_w, ffn_conv_b,
              ffn_w_down):
    att_caches = [(cache_k_l0, cache_v_l0), (cache_k_l2, cache_v_l2)]
    bp = x_prompt.shape[0]
    xp, xs = x_prompt, x_sample
    att_new = []
    rg_h_p, rg_h_s, rg_c_p, rg_c_s, ffn_c_p, ffn_c_s = [], [], [], [], [], []
    for layer in range(DEPTH):
        j = layer // N_MIXERS
        hp = _rmsnorm(xp, norm_mix[layer])
        hs = _rmsnorm(xs, norm_mix[layer])
        if layer % N_MIXERS == 0:
            ck, cv = att_caches[j]
            mp, kp, vp = _moba_mixer(hp, attn_w_qkv[j], attn_w_o[j], None, None)
            ms, k_s, v_s = _moba_mixer(hs, attn_w_qkv[j], attn_w_o[j],
                                       _gather_pages(ck, page_table), _gather_pages(cv, page_table))
            att_new.append((kp, vp, k_s, v_s))
        else:
            rg_w = (rg_w_gate[j], rg_w_x[j], rg_conv_w[j], rg_conv_b[j], rg_w_a[j], rg_b_a[j],
                    rg_w_i[j], rg_b_i[j], rg_lambda[j], rg_w_out[j])
            h0p = jnp.zeros((bp, D_RNN), state_rglru_h.dtype)
            c0p = jnp.zeros((bp, RG_CONV_W - 1, D_RNN), state_rglru_conv.dtype)
            mp, hl_p, cl_p = _rglru_mixer(hp, h0p, c0p, *rg_w)
            ms, hl_s, cl_s = _rglru_mixer(hs, state_rglru_h[j], state_rglru_conv[j], *rg_w)
            rg_h_p.append(hl_p)
            rg_h_s.append(hl_s)
            rg_c_p.append(cl_p)
            rg_c_s.append(cl_s)
        xp = xp + mp
        xs = xs + ms
        hp = _rmsnorm(xp, norm_ffn[layer])
        hs = _rmsnorm(xs, norm_ffn[layer])
        fw = (ffn_w_gate[layer], ffn_w_up[layer], ffn_conv_w[layer], ffn_conv_b[layer], ffn_w_down[layer])
        f0p = jnp.zeros((bp, FFN_CONV_W - 1, D_FF), state_ffn_conv.dtype)
        fp, fb_p = _conv_ffn(hp, f0p, *fw)
        fs, fb_s = _conv_ffn(hs, state_ffn_conv[layer], *fw)
        ffn_c_p.append(fb_p)
        ffn_c_s.append(fb_s)
        xp = xp + fp
        xs = xs + fs
    y_prompt = _rmsnorm(xp, norm_final)
    y_sample = _rmsnorm(xs, norm_final)
    (k_l0_prompt, v_l0_prompt, k_l0_sample, v_l0_sample), (k_l2_prompt, v_l2_prompt, k_l2_sample, v_l2_sample) = att_new
    return (y_prompt, y_sample,
            k_l0_prompt, v_l0_prompt, k_l0_sample, v_l0_sample,
            k_l2_prompt, v_l2_prompt, k_l2_sample, v_l2_sample,
            jnp.stack(rg_h_p), jnp.stack(rg_h_s), jnp.stack(rg_c_p), jnp.stack(rg_c_s),
            jnp.stack(ffn_c_p), jnp.stack(ffn_c_s))
```

```python
import functools
import math

import jax
import jax.numpy as jnp
from jax import lax
from jax.experimental import pallas as pl
from jax.experimental.pallas import tpu as pltpu

F32 = jnp.float32
BF16 = jnp.bfloat16
HIGHEST = lax.Precision.HIGHEST

N_HEADS = 16
MOBA_BLOCK = 256
BLOCK_SHIFT = MOBA_BLOCK.bit_length() - 1
MOBA_TOP_K = 3
RG_C = 8.0
NORM_EPS = 1e-6

V7X_VMEM_BYTES = 64 * 1024 * 1024
VMEM_LIMIT = V7X_VMEM_BYTES - 8 * 1024 * 1024
LANES = 128
SUBLANES = 8

ROW_TILE = 512


def _params(n_axes):
    return pltpu.CompilerParams(
        dimension_semantics=("arbitrary",) * n_axes, vmem_limit_bytes=VMEM_LIMIT)


def _norm_kernel(xp_ref, xs_ref, g_ref, op_ref, os_ref, *, nm):
    m = pl.program_id(0)
    g = g_ref[...]

    def nrm(x):
        y = x * lax.rsqrt(jnp.mean(x * x, axis=-1, keepdims=True) + NORM_EPS)
        return y * g

    @pl.when(m < nm)
    def _():
        op_ref[...] = nrm(xp_ref[...]).astype(op_ref.dtype)

    @pl.when(m == nm)
    def _():
        os_ref[...] = nrm(xs_ref[...]).astype(os_ref.dtype)


def _rmsnorm(xp, xs, gains, layer, out_dtype):
    mp, d = xp.shape
    ms = xs.shape[0]
    nm = mp // ROW_TILE
    last = nm - 1
    return pl.pallas_call(
        functools.partial(_norm_kernel, nm=nm),
        grid=(nm + 1,),
        in_specs=[
            pl.BlockSpec((ROW_TILE, d), lambda m: (jnp.minimum(m, last), 0)),
            pl.BlockSpec((ms, d), lambda m: (0, 0)),
            pl.BlockSpec((None, 1, d), lambda m: (layer, 0, 0)),
        ],
        out_specs=[
            pl.BlockSpec((ROW_TILE, d), lambda m: (jnp.minimum(m, last), 0)),
            pl.BlockSpec((ms, d), lambda m: (0, 0)),
        ],
        out_shape=[jax.ShapeDtypeStruct((mp, d), out_dtype),
                   jax.ShapeDtypeStruct((ms, d), F32)],
        compiler_params=_params(1),
        name="rmsnorm",
    )(xp, xs, gains)


def _gelu_tanh(x):
    return jax.nn.gelu(x, approximate=True)


def _mm_kernel(*refs, nm, epilogue, has_res):
    if has_res:
        ap_ref, as_ref, w_ref, rp_ref, rs_ref, op_ref, os_ref, wb_ref = refs
    else:
        ap_ref, as_ref, w_ref, op_ref, os_ref, wb_ref = refs
        rp_ref = rs_ref = None
    m = pl.program_id(1)

    @pl.when(m == 0)
    def _():
        wb_ref[...] = w_ref[...].astype(BF16)

    def finish(acc, r_ref, o_ref):
        if epilogue == "gelu_tanh":
            acc = _gelu_tanh(acc)
        if r_ref is not None:
            acc = r_ref[...] + acc
        o_ref[...] = acc.astype(o_ref.dtype)

    @pl.when(m < nm)
    def _():
        acc = jnp.dot(ap_ref[...], wb_ref[...], preferred_element_type=F32)
        finish(acc, rp_ref, op_ref)

    @pl.when(m == nm)
    def _():
        acc = jnp.dot(as_ref[...].astype(BF16), wb_ref[...], preferred_element_type=F32)
        finish(acc, rs_ref, os_ref)


def _matmul(ap, a_s, w, layer, *, col_off=0, n_out, tn, epilogue=None, res=None,
            out_dtype=F32):
    mp, k = ap.shape
    ms = a_s.shape[0]
    nm = mp // ROW_TILE
    last = nm - 1
    nn = n_out // tn
    off = col_off // tn
    in_specs = [
        pl.BlockSpec((ROW_TILE, k), lambda n, m: (jnp.minimum(m, last), 0)),
        pl.BlockSpec((ms, k), lambda n, m: (0, 0)),
        pl.BlockSpec((None, k, tn), lambda n, m: (layer, 0, off + n)),
    ]
    args = [ap, a_s, w]
    if res is not None:
        in_specs += [
            pl.BlockSpec((ROW_TILE, tn), lambda n, m: (jnp.minimum(m, last), n)),
            pl.BlockSpec((ms, tn), lambda n, m: (0, n)),
        ]
        args += list(res)
    return pl.pallas_call(
        functools.partial(_mm_kernel, nm=nm, epilogue=epilogue, has_res=res is not None),
        grid=(nn, nm + 1),
        in_specs=in_specs,
        out_specs=[
            pl.BlockSpec((ROW_TILE, tn), lambda n, m: (jnp.minimum(m, last), n)),
            pl.BlockSpec((ms, tn), lambda n, m: (0, n)),
        ],
        out_shape=[jax.ShapeDtypeStruct((mp, n_out), out_dtype),
                   jax.ShapeDtypeStruct((ms, n_out), F32)],
        scratch_shapes=[pltpu.VMEM((k, tn), BF16)],
        compiler_params=_params(2),
        name="matmul",
    )(*args)


def _moba_prompt_kernel(slopes_ref, q_ref, k_ref, v_ref, o_ref, *, seq, hd):
    nblk = seq // MOBA_BLOCK
    slope = slopes_ref[pl.program_id(1)]
    scale = 1.0 / math.sqrt(hd)
    q = q_ref[...]
    k = k_ref[...]
    qb = q.astype(BF16)
    kb = k.astype(BF16)
    vb = v_ref[...].astype(BF16)

    kmean = jnp.mean(k.reshape(nblk, MOBA_BLOCK, hd), axis=1)
    gate = lax.dot_general(kmean, q, (((1,), (1,)), ((), ())), precision=HIGHEST,
                           preferred_element_type=F32)
    n_idx = lax.broadcasted_iota(jnp.int32, (nblk, seq), 0)
    own = lax.broadcasted_iota(jnp.int32, (nblk, seq), 1) >> BLOCK_SHIFT
    g = jnp.where(n_idx < own, gate, -jnp.inf)
    sel = n_idx == own
    for _ in range(MOBA_TOP_K):
        mx = jnp.max(g, axis=0, keepdims=True)
        first = jnp.min(jnp.where(g == mx, n_idx, nblk), axis=0, keepdims=True)
        pick = (n_idx == first) & (mx > -jnp.inf)
        sel = sel | pick
        g = jnp.where(pick, -jnp.inf, g)

    sel_pad = jnp.concatenate(
        [jnp.where(sel, 1.0, 0.0), jnp.zeros((LANES - nblk, seq), F32)], axis=0)
    sel_t = sel_pad.T.astype(BF16)
    expand = jnp.where(
        lax.broadcasted_iota(jnp.int32, (LANES, seq), 0)
        == lax.broadcasted_iota(jnp.int32, (LANES, seq), 1) >> BLOCK_SHIFT, 1.0, 0.0).astype(BF16)
    rel = (lax.broadcasted_iota(jnp.int32, (MOBA_BLOCK, seq), 0)
           - lax.broadcasted_iota(jnp.int32, (MOBA_BLOCK, seq), 1)).astype(F32)

    for i in range(nblk):
        lo, hi = i * MOBA_BLOCK, (i + 1) * MOBA_BLOCK
        s = lax.dot_general(qb[lo:hi], kb[:hi], (((1,), (1,)), ((), ())),
                            preferred_element_type=F32)
        chosen = jnp.dot(sel_t[lo:hi], expand[:, :hi], preferred_element_type=F32)
        dist = rel[:, :hi] + float(lo)
        logit = s * scale - slope * dist
        logit = jnp.where((chosen > 0.5) & (dist >= 0.0), logit, -jnp.inf)
        mx = jnp.max(logit, axis=1, keepdims=True)
        p = jnp.exp(logit - mx)
        denom = jnp.sum(p, axis=1, keepdims=True)
        o = jnp.dot(p.astype(BF16), vb[:hi], preferred_element_type=F32) / denom
        o_ref[lo:hi, :] = o.astype(o_ref.dtype)


def _moba_prompt(slopes, q, k, v, batch, seq):
    d = q.shape[1]
    hd = d // N_HEADS
    spec = pl.BlockSpec((seq, hd), lambda b, h: (b, h))
    return pl.pallas_call(
        functools.partial(_moba_prompt_kernel, seq=seq, hd=hd),
        grid=(batch, N_HEADS),
        in_specs=[pl.BlockSpec(memory_space=pltpu.SMEM), spec, spec, spec],
        out_specs=spec,
        out_shape=jax.ShapeDtypeStruct((batch * seq, d), BF16),
        compiler_params=_params(2),
        name="moba_prompt",
    )(slopes, q, k, v)


def _kmean_kernel(pt_ref, ka_ref, kb_ref, o_ref, *, rows):
    n = pl.program_id(1)
    s = (jnp.sum(ka_ref[...], axis=0, keepdims=True)
         + jnp.sum(kb_ref[...], axis=0, keepdims=True))
    o_ref[pl.ds(n, 1), :] = s * (1.0 / rows)


def _page_kmean(cache, page_table):
    _, page, d = cache.shape
    nseq, n_pages = page_table.shape
    per_blk = MOBA_BLOCK // page
    assert per_blk == 2 and n_pages % per_blk == 0
    nblk = n_pages // per_blk
    grid_spec = pltpu.PrefetchScalarGridSpec(
        num_scalar_prefetch=1,
        grid=(nseq, nblk),
        in_specs=[
            pl.BlockSpec((None, page, d), lambda s, n, pt: (pt[s, 2 * n], 0, 0)),
            pl.BlockSpec((None, page, d), lambda s, n, pt: (pt[s, 2 * n + 1], 0, 0)),
        ],
        out_specs=pl.BlockSpec((None, nblk, d), lambda s, n, pt: (s, 0, 0)),
    )
    return pl.pallas_call(
        functools.partial(_kmean_kernel, rows=MOBA_BLOCK),
        grid_spec=grid_spec,
        out_shape=jax.ShapeDtypeStruct((nseq, nblk, d), F32),
        compiler_params=_params(2),
        name="page_kmean",
    )(page_table, cache, cache)


def _select_kernel(q_ref, km_ref, o_ref, *, hd):
    nblk, d = km_ref.shape
    prod = km_ref[...] * q_ref[...]
    head_of = jnp.where(
        lax.broadcasted_iota(jnp.int32, (d, LANES), 0) >> (hd.bit_length() - 1)
        == lax.broadcasted_iota(jnp.int32, (d, LANES), 1), 1.0, 0.0)
    g = jnp.dot(prod, head_of, precision=HIGHEST, preferred_element_type=F32)
    n_idx = lax.broadcasted_iota(jnp.int32, (nblk, LANES), 0)
    rows = []
    for _ in range(MOBA_TOP_K):
        mx = jnp.max(g, axis=0, keepdims=True)
        first = jnp.min(jnp.where(g == mx, n_idx, nblk), axis=0, keepdims=True)
        rows.append(first)
        g = jnp.where(n_idx == first, -jnp.inf, g)
    rows.append(jnp.zeros((SUBLANES - MOBA_TOP_K, LANES), jnp.int32))
    o_ref[...] = jnp.concatenate(rows, axis=0)


def _select_blocks(q_s, kmean):
    nseq, nblk, d = kmean.shape
    return pl.pallas_call(
        functools.partial(_select_kernel, hd=d // N_HEADS),
        grid=(nseq,),
        in_specs=[pl.BlockSpec((None, 1, d), lambda s: (s, 0, 0)),
                  pl.BlockSpec((None, nblk, d), lambda s: (s, 0, 0))],
        out_specs=pl.BlockSpec((None, SUBLANES, LANES), lambda s: (s, 0, 0)),
        out_shape=jax.ShapeDtypeStruct((nseq, SUBLANES, LANES), jnp.int32),
        compiler_params=_params(1),
        name="select_blocks",
    )(q_s, kmean)


def _moba_sample_kernel(pt_ref, idx_ref, slopes_ref, q_ref, kn_ref, vn_ref, *refs,
                        page, past_len, hd):
    n_pg = MOBA_TOP_K * (MOBA_BLOCK // page)
    k_refs, v_refs, o_ref = refs[:n_pg], refs[n_pg:2 * n_pg], refs[2 * n_pg]
    s_id = pl.program_id(0)
    h = pl.program_id(1)
    slope = slopes_ref[h]
    scale = 1.0 / math.sqrt(hd)
    q = q_ref[...]
    q8 = jnp.broadcast_to(q, (SUBLANES, hd))
    lane = lax.broadcasted_iota(jnp.int32, (1, page), 1)
    logits = []
    for t in range(n_pg):
        r, half = divmod(t, MOBA_BLOCK // page)
        blk = idx_ref[(s_id * N_HEADS + h) * MOBA_TOP_K + r]
        s = lax.dot_general(q8, k_refs[t][...], (((1,), (1,)), ((), ())), precision=HIGHEST,
                            preferred_element_type=F32)[0:1]
        dist = (past_len - blk * MOBA_BLOCK - half * page - lane).astype(F32)
        logits.append(s * scale - slope * dist)
    l_self = jnp.sum(q * kn_ref[...], axis=1, keepdims=True) * scale
    mx = l_self
    for l in logits:
        mx = jnp.maximum(mx, jnp.max(l, axis=1, keepdims=True))
    p_self = jnp.exp(l_self - mx)
    denom = p_self
    acc = p_self * vn_ref[...]
    for t in range(n_pg):
        p = jnp.exp(logits[t] - mx)
        denom = denom + jnp.sum(p, axis=1, keepdims=True)
        acc = acc + jnp.dot(jnp.broadcast_to(p, (SUBLANES, page)), v_refs[t][...],
                            precision=HIGHEST, preferred_element_type=F32)[0:1]
    o_ref[...] = acc / denom


def _moba_sample(page_table, top_idx, slopes, q_s, k_s, v_s, cache_k, cache_v, past_len):
    _, page, d = cache_k.shape
    nseq = q_s.shape[0]
    hd = d // N_HEADS
    per_blk = MOBA_BLOCK // page
    n_pg = MOBA_TOP_K * per_blk

    def page_spec(t):
        r, half = divmod(t, per_blk)
        return pl.BlockSpec(
            (None, page, hd),
            lambda s, h, pt, idx: (
                pt[s, per_blk * idx[(s * N_HEADS + h) * MOBA_TOP_K + r] + half], 0, h))

    row_spec = pl.BlockSpec((None, 1, hd), lambda s, h, pt, idx: (s, 0, h))
    grid_spec = pltpu.PrefetchScalarGridSpec(
        num_scalar_prefetch=2,
        grid=(nseq, N_HEADS),
        in_specs=([pl.BlockSpec(memory_space=pltpu.SMEM), row_spec, row_spec, row_spec]
                  + [page_spec(t) for t in range(n_pg)]
                  + [page_spec(t) for t in range(n_pg)]),
        out_specs=row_spec,
    )
    return pl.pallas_call(
        functools.partial(_moba_sample_kernel, page=page, past_len=past_len, hd=hd),
        grid_spec=grid_spec,
        out_shape=jax.ShapeDtypeStruct((nseq, 1, d), F32),
        compiler_params=_params(2),
        name="moba_sample",
    )(page_table, top_idx, slopes, q_s, k_s, v_s, *([cache_k] * n_pg), *([cache_v] * n_pg))


def _rg_gates(xc, wa_ref, ba_ref, wi_ref, bi_ref, lam_ref):
    xcb = xc.astype(BF16)
    r = jax.nn.sigmoid(
        jnp.dot(xcb, wa_ref[...].astype(BF16), preferred_element_type=F32) + ba_ref[...])
    i = jax.nn.sigmoid(
        jnp.dot(xcb, wi_ref[...].astype(BF16), preferred_element_type=F32) + bi_ref[...])
    neg_lam = -lam_ref[...]
    softplus = jnp.maximum(neg_lam, 0.0) + jnp.log1p(jnp.exp(-jnp.abs(neg_lam)))
    log_a = -RG_C * r * softplus
    a = jnp.exp(log_a)
    mult = jnp.sqrt(jnp.tanh(-log_a) * (a * a + 1.0))
    return a, mult * (i * xc)


def _rglru_kernel(xr_ref, yg_ref, xrs_ref, ygs_ref, cs_ref, h0_ref, cw_ref, cb_ref,
                  wa_ref, ba_ref, wi_ref, bi_ref, lam_ref,
                  o_ref, hl_ref, os_ref, hs_ref, xs_scr, a_scr, b_scr, *, nb, seq):
    b = pl.program_id(1)
    cw = cw_ref[...]
    width = cw.shape[0]
    gate_refs = (wa_ref, ba_ref, wi_ref, bi_ref, lam_ref)

    @pl.when(b < nb)
    def _():
        x = xr_ref[...]
        xs_scr[0:SUBLANES, :] = jnp.zeros((SUBLANES, x.shape[1]), F32)
        xs_scr[SUBLANES:, :] = x
        xc = xs_scr[pl.ds(SUBLANES - (width - 1), seq), :] * cw[0:1]
        for j in range(1, width):
            xc = xc + xs_scr[pl.ds(SUBLANES - (width - 1) + j, seq), :] * cw[j:j + 1]
        xc = xc + cb_ref[...]
        a, bx = _rg_gates(xc, *gate_refs)
        a_scr[...] = a
        b_scr[...] = bx
        row = lax.broadcasted_iota(jnp.int32, (SUBLANES, x.shape[1]), 0)

        def body(g, h):
            off = pl.multiple_of(g * SUBLANES, SUBLANES)
            aa = a_scr[pl.ds(off, SUBLANES), :]
            bb = b_scr[pl.ds(off, SUBLANES), :]
            for d in (1, 2, 4):
                a_sh = jnp.where(row >= d, pltpu.roll(aa, d, 0), 1.0)
                b_sh = jnp.where(row >= d, pltpu.roll(bb, d, 0), 0.0)
                bb = aa * b_sh + bb
                aa = aa * a_sh
            hh = aa * h + bb
            b_scr[pl.ds(off, SUBLANES), :] = hh
            return hh[SUBLANES - 1:SUBLANES, :]

        h_last = lax.fori_loop(0, seq // SUBLANES, body, jnp.zeros((1, x.shape[1]), F32))
        hl_ref[...] = h_last
        o_ref[...] = (b_scr[...] * yg_ref[...]).astype(o_ref.dtype)

    @pl.when(b == nb)
    def _():
        xc = cs_ref[0] * cw[0:1]
        for j in range(1, width - 1):
            xc = xc + cs_ref[j] * cw[j:j + 1]
        xc = xc + xrs_ref[...] * cw[width - 1:width] + cb_ref[...]
        a, bx = _rg_gates(xc, *gate_refs)
        hh = a * h0_ref[...] + bx
        hs_ref[...] = hh
        os_ref[...] = hh * ygs_ref[...]


def _rglru(xr, yg, xr_s, yg_s, conv_state, h0, conv_w, conv_b, w_a, b_a, w_i, b_i, lam, j,
           batch, seq):
    d = xr.shape[1]
    ms = xr_s.shape[0]
    nblk, cw = w_a.shape[1], w_a.shape[2]
    width = conv_w.shape[1]
    last = batch - 1
    seq_spec = pl.BlockSpec((seq, cw), lambda c, b: (jnp.minimum(b, last), c))
    smp_spec = pl.BlockSpec((ms, cw), lambda c, b: (0, c))
    vec_spec = pl.BlockSpec((None, 1, cw), lambda c, b: (j, 0, c))
    mat_spec = pl.BlockSpec((None, None, cw, cw), lambda c, b: (j, c, 0, 0))
    return pl.pallas_call(
        functools.partial(_rglru_kernel, nb=batch, seq=seq),
        grid=(nblk, batch + 1),
        in_specs=[
            seq_spec, seq_spec, smp_spec, smp_spec,
            pl.BlockSpec((width - 1, ms, cw), lambda c, b: (0, 0, c)),
            pl.BlockSpec((None, ms, cw), lambda c, b: (j, 0, c)),
            pl.BlockSpec((None, width, cw), lambda c, b: (j, 0, c)),
            vec_spec, mat_spec, vec_spec, mat_spec, vec_spec, vec_spec,
        ],
        out_specs=[
            seq_spec,
            pl.BlockSpec((None, 1, cw), lambda c, b: (jnp.minimum(b, last), 0, c)),
            smp_spec, smp_spec,
        ],
        out_shape=[
            jax.ShapeDtypeStruct((batch * seq, d), BF16),
            jax.ShapeDtypeStruct((batch, 1, d), F32),
            jax.ShapeDtypeStruct((ms, d), F32),
            jax.ShapeDtypeStruct((ms, d), F32),
        ],
        scratch_shapes=[pltpu.VMEM((seq + SUBLANES, cw), F32),
                        pltpu.VMEM((seq, cw), F32),
                        pltpu.VMEM((seq, cw), F32)],
        compiler_params=_params(2),
        name="rglru",
    )(xr, yg, xr_s, yg_s, conv_state, h0, conv_w, conv_b, w_a, b_a, w_i, b_i, lam)


def _gelu_erf(x):
    return 0.5 * x * (1.0 + lax.erf(x * math.sqrt(0.5)))


def _ffn_up_kernel(ap_ref, as_ref, wg_ref, wu_ref, s_ref, cw_ref, cb_ref,
                   hp_ref, hs_ref, tail_ref, gs_ref, wgb_ref, wub_ref, g_scr, *, nm, tiles_per_seq):
    m = pl.program_id(1)
    cw = cw_ref[...]
    width = cw.shape[0]
    tm = ap_ref.shape[0]

    @pl.when(m == 0)
    def _():
        wgb_ref[...] = wg_ref[...].astype(BF16)
        wub_ref[...] = wu_ref[...].astype(BF16)

    @pl.when(m < nm)
    def _():
        a = ap_ref[...]
        g = jnp.dot(a, wgb_ref[...], preferred_element_type=F32)
        u = jnp.dot(a, wub_ref[...], preferred_element_type=F32)

        @pl.when(m % tiles_per_seq == 0)
        def _():
            g_scr[0:SUBLANES, :] = jnp.zeros((SUBLANES, g.shape[1]), F32)

        @pl.when(m % tiles_per_seq != 0)
        def _():
            g_scr[0:SUBLANES, :] = g_scr[tm:tm + SUBLANES, :]

        g_scr[SUBLANES:, :] = g
        y = g_scr[pl.ds(SUBLANES - (width - 1), tm), :] * cw[0:1]
        for j in range(1, width - 1):
            y = y + g_scr[pl.ds(SUBLANES - (width - 1) + j, tm), :] * cw[j:j + 1]
        y = y + g * cw[width - 1:width] + cb_ref[...]
        hp_ref[...] = (_gelu_erf(y) * u).astype(hp_ref.dtype)

        @pl.when(m % tiles_per_seq == tiles_per_seq - 1)
        def _():
            tail_ref[...] = g[tm - SUBLANES:, :]

    @pl.when(m == nm)
    def _():
        a = as_ref[...].astype(BF16)
        g = jnp.dot(a, wgb_ref[...], preferred_element_type=F32)
        u = jnp.dot(a, wub_ref[...], preferred_element_type=F32)
        y = s_ref[0] * cw[0:1]
        for j in range(1, width - 1):
            y = y + s_ref[j] * cw[j:j + 1]
        y = y + g * cw[width - 1:width] + cb_ref[...]
        hs_ref[...] = _gelu_erf(y) * u
        gs_ref[...] = g


def _ffn_up(ap, a_s, w_gate, w_up, state, conv_w, conv_b, layer, batch, seq, tn):
    mp, k = ap.shape
    ms = a_s.shape[0]
    f = w_gate.shape[2]
    width = conv_w.shape[1]
    nm = mp // ROW_TILE
    last = nm - 1
    tps = seq // ROW_TILE
    w_spec = pl.BlockSpec((None, k, tn), lambda n, m: (layer, 0, n))
    return pl.pallas_call(
        functools.partial(_ffn_up_kernel, nm=nm, tiles_per_seq=tps),
        grid=(f // tn, nm + 1),
        in_specs=[
            pl.BlockSpec((ROW_TILE, k), lambda n, m: (jnp.minimum(m, last), 0)),
            pl.BlockSpec((ms, k), lambda n, m: (0, 0)),
            w_spec, w_spec,
            pl.BlockSpec((width - 1, ms, tn), lambda n, m: (0, 0, n)),
            pl.BlockSpec((None, width, tn), lambda n, m: (layer, 0, n)),
            pl.BlockSpec((None, 1, tn), lambda n, m: (layer, 0, n)),
        ],
        out_specs=[
            pl.BlockSpec((ROW_TILE, tn), lambda n, m: (jnp.minimum(m, last), n)),
            pl.BlockSpec((ms, tn), lambda n, m: (0, n)),
            pl.BlockSpec((None, SUBLANES, tn), lambda n, m: (jnp.minimum(m, last) // tps, 0, n)),
            pl.BlockSpec((ms, tn), lambda n, m: (0, n)),
        ],
        out_shape=[
            jax.ShapeDtypeStruct((mp, f), BF16),
            jax.ShapeDtypeStruct((ms, f), F32),
            jax.ShapeDtypeStruct((batch, SUBLANES, f), F32),
            jax.ShapeDtypeStruct((ms, f), F32),
        ],
        scratch_shapes=[pltpu.VMEM((k, tn), BF16), pltpu.VMEM((k, tn), BF16),
                        pltpu.VMEM((ROW_TILE + SUBLANES, tn), F32)],
        compiler_params=_params(2),
        name="ffn_up",
    )(ap, a_s, w_gate, w_up, state, conv_w, conv_b)


def kernel(x_prompt, x_sample, cache_k_l0, cache_v_l0, cache_k_l2, cache_v_l2, page_table, state_rglru_h, state_rglru_conv, state_ffn_conv, norm_mix, norm_ffn, norm_final, attn_w_qkv, attn_w_o, rg_w_gate, rg_w_x, rg_conv_w, rg_conv_b, rg_w_a, rg_b_a, rg_w_i, rg_b_i, rg_lambda, rg_w_out, ffn_w_gate, ffn_w_up, ffn_conv_w, ffn_conv_b, ffn_w_down):
    batch, seq, d = x_prompt.shape
    nseq, dec_seq, _ = x_sample.shape
    assert dec_seq == 1 and seq % ROW_TILE == 0 and seq % MOBA_BLOCK == 0
    depth = norm_mix.shape[0]
    hd = d // N_HEADS
    n_pool, page = cache_k_l0.shape[:2]
    past_len = page_table.shape[1] * page
    assert past_len % MOBA_BLOCK == 0
    d_ff = ffn_w_gate.shape[2]
    mp = batch * seq

    xp = x_prompt.reshape(mp, d)
    xs = x_sample.reshape(nseq, d)
    slopes = jnp.exp2(-8.0 * jnp.arange(1, N_HEADS + 1, dtype=F32) / N_HEADS)
    caches = [(cache_k_l0.reshape(n_pool, page, d), cache_v_l0.reshape(n_pool, page, d)),
              (cache_k_l2.reshape(n_pool, page, d), cache_v_l2.reshape(n_pool, page, d))]
    norm_mix3 = norm_mix.reshape(depth, 1, d)
    norm_ffn3 = norm_ffn.reshape(depth, 1, d)
    rg_conv_b3 = rg_conv_b.reshape(-1, 1, d)
    rg_b_a3 = rg_b_a.reshape(-1, 1, d)
    rg_b_i3 = rg_b_i.reshape(-1, 1, d)
    rg_lam3 = rg_lambda.reshape(-1, 1, d)
    ffn_conv_b3 = ffn_conv_b.reshape(depth, 1, d_ff)

    att_out, rg_h_p, rg_h_s, rg_c_p, rg_c_s, ffn_c_p, ffn_c_s = [], [], [], [], [], [], []
    for layer in range(depth):
        j = layer // 2
        hp, hs = _rmsnorm(xp, xs, norm_mix3, layer, BF16)
        if layer % 2 == 0:
            ck, cv = caches[j]
            qp, qs = _matmul(hp, hs, attn_w_qkv, j, col_off=0, n_out=d, tn=1024)
            kp, ks = _matmul(hp, hs, attn_w_qkv, j, col_off=d, n_out=d, tn=1024)
            vp, vs = _matmul(hp, hs, attn_w_qkv, j, col_off=2 * d, n_out=d, tn=1024)
            op = _moba_prompt(slopes, qp, kp, vp, batch, seq)
            qs3, ks3, vs3 = (t.reshape(nseq, 1, d) for t in (qs, ks, vs))
            kmean = _page_kmean(ck, page_table)
            picks = _select_blocks(qs3, kmean)
            top_idx = picks[:, :MOBA_TOP_K, :N_HEADS].transpose(0, 2, 1).reshape(-1)
            o_s = _moba_sample(page_table, top_idx, slopes, qs3, ks3, vs3, ck, cv, past_len)
            xp, xs = _matmul(op, o_s.reshape(nseq, d), attn_w_o, j, n_out=d, tn=1024,
                             res=(xp, xs))
            att_out += [kp.reshape(batch, seq, N_HEADS, hd), vp.reshape(batch, seq, N_HEADS, hd),
                        ks.reshape(nseq, 1, N_HEADS, hd), vs.reshape(nseq, 1, N_HEADS, hd)]
        else:
            ygp, ygs = _matmul(hp, hs, rg_w_gate, j, n_out=d, tn=1024, epilogue="gelu_tanh")
            xrp, xrs = _matmul(hp, hs, rg_w_x, j, n_out=d, tn=1024)
            conv_state = state_rglru_conv[j].transpose(1, 0, 2)
            gp, hl_p, gs, hl_s = _rglru(xrp, ygp, xrs, ygs, conv_state, state_rglru_h,
                                        rg_conv_w, rg_conv_b3, rg_w_a, rg_b_a3, rg_w_i, rg_b_i3,
                                        rg_lam3, j, batch, seq)
            xp, xs = _matmul(gp, gs, rg_w_out, j, n_out=d, tn=1024, res=(xp, xs))
            keep = rg_conv_w.shape[1] - 1
            rg_h_p.append(hl_p.reshape(batch, d))
            rg_h_s.append(hl_s)
            rg_c_p.append(xrp.reshape(batch, seq, d)[:, seq - keep:])
            rg_c_s.append(jnp.concatenate([state_rglru_conv[j][:, 1:], xrs[:, None, :]], axis=1))
        hp, hs = _rmsnorm(xp, xs, norm_ffn3, layer, BF16)
        f_state = state_ffn_conv[layer].transpose(1, 0, 2)
        up_p, up_s, tail, g_s = _ffn_up(hp, hs, ffn_w_gate, ffn_w_up, f_state, ffn_conv_w,
                                        ffn_conv_b3, layer, batch, seq, tn=512)
        xp, xs = _matmul(up_p, up_s, ffn_w_down, layer, n_out=d, tn=512, res=(xp, xs))
        keep = ffn_conv_w.shape[1] - 1
        ffn_c_p.append(tail[:, SUBLANES - keep:])
        ffn_c_s.append(jnp.concatenate([state_ffn_conv[layer][:, 1:], g_s[:, None, :]], axis=1))

    yp, ys = _rmsnorm(xp, xs, norm_final.reshape(1, 1, d), 0, F32)
    return (yp.reshape(batch, seq, d), ys.reshape(nseq, 1, d), *att_out,
            jnp.stack(rg_h_p), jnp.stack(rg_h_s), jnp.stack(rg_c_p), jnp.stack(rg_c_s),
            jnp.stack(ffn_c_p), jnp.stack(ffn_c_s))
```

```python
import functools
import math

import jax
import jax.numpy as jnp
from jax import lax
from jax.experimental import pallas as pl
from jax.experimental.pallas import tpu as pltpu

F32 = jnp.float32
BF16 = jnp.bfloat16
HIGHEST = lax.Precision.HIGHEST

N_HEADS = 16
MOBA_BLOCK = 256
BLOCK_SHIFT = MOBA_BLOCK.bit_length() - 1
MOBA_TOP_K = 3
RG_C = 8.0
NORM_EPS = 1e-6
MASK_BIG = 2.0 ** 100
LOG2_E = math.log2(math.e)

V7X_VMEM_BYTES = 64 * 1024 * 1024
VMEM_LIMIT = V7X_VMEM_BYTES - 8 * 1024 * 1024
LANES = 128
SUBLANES = 8

NORM_ROWS = 512
MM_ROWS = 1024
KMEAN_PAGES = 4


def _params(n_axes):
    return pltpu.CompilerParams(
        dimension_semantics=("arbitrary",) * n_axes, vmem_limit_bytes=VMEM_LIMIT)


def _norm_kernel(xp_ref, xs_ref, g_ref, op_ref, os_ref, *, nm):
    m = pl.program_id(0)
    g = g_ref[...]

    def nrm(x):
        y = x * lax.rsqrt(jnp.mean(x * x, axis=-1, keepdims=True) + NORM_EPS)
        return y * g

    @pl.when(m < nm)
    def _():
        op_ref[...] = nrm(xp_ref[...]).astype(op_ref.dtype)

    @pl.when(m == nm)
    def _():
        os_ref[...] = nrm(xs_ref[...]).astype(os_ref.dtype)


def _rmsnorm(xp, xs, gains, layer, out_dtype):
    mp, d = xp.shape
    ms = xs.shape[0]
    nm = mp // NORM_ROWS
    last = nm - 1
    return pl.pallas_call(
        functools.partial(_norm_kernel, nm=nm),
        grid=(nm + 1,),
        in_specs=[
            pl.BlockSpec((NORM_ROWS, d), lambda m: (jnp.minimum(m, last), 0)),
            pl.BlockSpec((ms, d), lambda m: (0, 0)),
            pl.BlockSpec((None, 1, d), lambda m: (layer, 0, 0)),
        ],
        out_specs=[
            pl.BlockSpec((NORM_ROWS, d), lambda m: (jnp.minimum(m, last), 0)),
            pl.BlockSpec((ms, d), lambda m: (0, 0)),
        ],
        out_shape=[jax.ShapeDtypeStruct((mp, d), out_dtype),
                   jax.ShapeDtypeStruct((ms, d), F32)],
        compiler_params=_params(1),
        name="rmsnorm",
    )(xp, xs, gains)


def _gelu_tanh(x):
    return jax.nn.gelu(x, approximate=True)


def _mm_kernel(*refs, nm, epilogue, has_res):
    if has_res:
        ap_ref, as_ref, w_ref, rp_ref, rs_ref, op_ref, os_ref, wb_ref = refs
    else:
        ap_ref, as_ref, w_ref, op_ref, os_ref, wb_ref = refs
        rp_ref = rs_ref = None
    m = pl.program_id(1)

    @pl.when(m == 0)
    def _():
        wb_ref[...] = w_ref[...].astype(BF16)

    def finish(acc, r_ref, o_ref):
        if epilogue == "gelu_tanh":
            acc = _gelu_tanh(acc)
        if r_ref is not None:
            acc = r_ref[...] + acc
        o_ref[...] = acc.astype(o_ref.dtype)

    @pl.when(m < nm)
    def _():
        acc = jnp.dot(ap_ref[...], wb_ref[...], preferred_element_type=F32)
        finish(acc, rp_ref, op_ref)

    @pl.when(m == nm)
    def _():
        acc = jnp.dot(as_ref[...].astype(BF16), wb_ref[...], preferred_element_type=F32)
        finish(acc, rs_ref, os_ref)


def _matmul(ap, a_s, w, layer, *, col_off=0, n_out, tn, epilogue=None, res=None,
            out_dtype=F32, w_buffers=2):
    mp, k = ap.shape
    ms = a_s.shape[0]
    tm = MM_ROWS
    nm = mp // tm
    last = nm - 1
    nn = n_out // tn
    off = col_off // tn
    in_specs = [
        pl.BlockSpec((tm, k), lambda n, m: (jnp.minimum(m, last), 0)),
        pl.BlockSpec((ms, k), lambda n, m: (0, 0)),
        pl.BlockSpec((None, k, tn), lambda n, m: (layer, 0, off + n),
                     pipeline_mode=pl.Buffered(w_buffers)),
    ]
    args = [ap, a_s, w]
    if res is not None:
        in_specs += [
            pl.BlockSpec((tm, tn), lambda n, m: (jnp.minimum(m, last), n)),
            pl.BlockSpec((ms, tn), lambda n, m: (0, n)),
        ]
        args += list(res)
    return pl.pallas_call(
        functools.partial(_mm_kernel, nm=nm, epilogue=epilogue, has_res=res is not None),
        grid=(nn, nm + 1),
        in_specs=in_specs,
        out_specs=[
            pl.BlockSpec((tm, tn), lambda n, m: (jnp.minimum(m, last), n)),
            pl.BlockSpec((ms, tn), lambda n, m: (0, n)),
        ],
        out_shape=[jax.ShapeDtypeStruct((mp, n_out), out_dtype),
                   jax.ShapeDtypeStruct((ms, n_out), F32)],
        scratch_shapes=[pltpu.VMEM((k, tn), BF16)],
        compiler_params=_params(2),
        name="matmul",
    )(*args)


def _moba_prompt_kernel(slopes_ref, q_ref, k_ref, v_ref, o_ref, qa_scr, ka_scr, vb_scr, t_scr,
                        *, seq, hd):
    nblk = seq // MOBA_BLOCK
    slope = slopes_ref[pl.program_id(1)]
    scale = 1.0 / math.sqrt(hd)
    q = q_ref[...]
    k = k_ref[...]

    kmean = jnp.mean(k.reshape(nblk, MOBA_BLOCK, hd), axis=1)
    gate = lax.dot_general(kmean, q, (((1,), (1,)), ((), ())), precision=HIGHEST,
                           preferred_element_type=F32)
    n_idx = lax.broadcasted_iota(jnp.int32, (nblk, seq), 0)
    own = lax.broadcasted_iota(jnp.int32, (nblk, seq), 1) >> BLOCK_SHIFT
    g = jnp.where(n_idx < own, gate, -jnp.inf)
    sel = n_idx == own
    for _ in range(MOBA_TOP_K):
        mx = jnp.max(g, axis=0, keepdims=True)
        first = jnp.min(jnp.where(g == mx, n_idx, nblk), axis=0, keepdims=True)
        pick = (n_idx == first) & (mx > -jnp.inf)
        sel = sel | pick
        g = jnp.where(pick, -jnp.inf, g)

    unsel_pad = jnp.concatenate(
        [jnp.where(sel, 0.0, 1.0), jnp.zeros((LANES - nblk, seq), F32)], axis=0)
    qa_scr[:, 0:hd] = q.astype(BF16)
    qa_scr[:, hd:] = unsel_pad.T.astype(BF16)
    ka_scr[:, 0:hd] = k.astype(BF16)
    ka_scr[:, hd:] = jnp.where(
        lax.broadcasted_iota(jnp.int32, (seq, LANES), 0) >> BLOCK_SHIFT
        == lax.broadcasted_iota(jnp.int32, (seq, LANES), 1), -MASK_BIG, 0.0).astype(BF16)
    vb_scr[...] = v_ref[...].astype(BF16)
    dist = (lax.broadcasted_iota(jnp.int32, (MOBA_BLOCK, seq), 0)
            - lax.broadcasted_iota(jnp.int32, (MOBA_BLOCK, seq), 1)
            + (seq - MOBA_BLOCK)).astype(F32)
    t_scr[...] = jnp.where(dist >= 0.0, (slope * LOG2_E) * dist, MASK_BIG)

    for i in range(nblk):
        lo, hi = i * MOBA_BLOCK, (i + 1) * MOBA_BLOCK
        s = lax.dot_general(qa_scr[lo:hi, :], ka_scr[0:hi, :], (((1,), (1,)), ((), ())),
                            preferred_element_type=F32)
        logit = s * (scale * LOG2_E) - t_scr[:, seq - hi:]
        mx = jnp.max(logit, axis=1, keepdims=True)
        p = jnp.exp2(logit - mx)
        denom = jnp.sum(p, axis=1, keepdims=True)
        o = jnp.dot(p.astype(BF16), vb_scr[0:hi, :], preferred_element_type=F32) / denom
        o_ref[lo:hi, :] = o.astype(o_ref.dtype)


def _moba_prompt(slopes, q, k, v, batch, seq):
    d = q.shape[1]
    hd = d // N_HEADS
    spec = pl.BlockSpec((seq, hd), lambda b, h: (b, h))
    return pl.pallas_call(
        functools.partial(_moba_prompt_kernel, seq=seq, hd=hd),
        grid=(batch, N_HEADS),
        in_specs=[pl.BlockSpec(memory_space=pltpu.SMEM), spec, spec, spec],
        out_specs=spec,
        out_shape=jax.ShapeDtypeStruct((batch * seq, d), BF16),
        scratch_shapes=[pltpu.VMEM((seq, hd + LANES), BF16), pltpu.VMEM((seq, hd + LANES), BF16),
                        pltpu.VMEM((seq, hd), BF16), pltpu.VMEM((MOBA_BLOCK, seq), F32)],
        compiler_params=_params(2),
        name="moba_prompt",
    )(slopes, q, k, v)


def _kmean_kernel(pt_ref, *refs, pages_per_blk):
    k_refs, o_ref = refs[:-1], refs[-1]
    n = pl.program_id(1)
    blks = len(k_refs) // pages_per_blk
    rows = pages_per_blk * k_refs[0].shape[0]
    for b in range(blks):
        s = jnp.sum(k_refs[b * pages_per_blk][...], axis=0)
        for p in range(1, pages_per_blk):
            s = s + jnp.sum(k_refs[b * pages_per_blk + p][...], axis=0)
        o_ref[n * blks + b] = s * (1.0 / rows)


def _page_kmean(cache, page_table):
    _, page, h, hd = cache.shape
    nseq, n_pages = page_table.shape
    per_blk = MOBA_BLOCK // page
    nblk = n_pages // per_blk
    assert MOBA_BLOCK % page == 0 and n_pages % KMEAN_PAGES == 0 and KMEAN_PAGES % per_blk == 0

    def page_spec(t):
        return pl.BlockSpec((None, page, h, hd),
                            lambda s, n, pt: (pt[s, KMEAN_PAGES * n + t], 0, 0, 0))

    grid_spec = pltpu.PrefetchScalarGridSpec(
        num_scalar_prefetch=1,
        grid=(nseq, n_pages // KMEAN_PAGES),
        in_specs=[page_spec(t) for t in range(KMEAN_PAGES)],
        out_specs=pl.BlockSpec((None, nblk, h, hd), lambda s, n, pt: (s, 0, 0, 0)),
    )
    return pl.pallas_call(
        functools.partial(_kmean_kernel, pages_per_blk=per_blk),
        grid_spec=grid_spec,
        out_shape=jax.ShapeDtypeStruct((nseq, nblk, h, hd), F32),
        compiler_params=_params(2),
        name="page_kmean",
    )(page_table, *([cache] * KMEAN_PAGES))


def _select_kernel(q_ref, km_ref, o_ref):
    nblk, h, _ = km_ref.shape
    g = jnp.sum(km_ref[...] * q_ref[...][None], axis=-1)
    n_idx = lax.broadcasted_iota(jnp.int32, (nblk, h), 0)
    rows = []
    for _ in range(MOBA_TOP_K):
        mx = jnp.max(g, axis=0, keepdims=True)
        first = jnp.min(jnp.where(g == mx, n_idx, nblk), axis=0, keepdims=True)
        rows.append(first)
        g = jnp.where(n_idx == first, -jnp.inf, g)
    rows.append(jnp.zeros((SUBLANES - MOBA_TOP_K, h), jnp.int32))
    o_ref[...] = jnp.concatenate(rows, axis=0)


def _select_blocks(q_s, kmean):
    nseq, nblk, h, hd = kmean.shape
    return pl.pallas_call(
        _select_kernel,
        grid=(nseq,),
        in_specs=[pl.BlockSpec((None, h, hd), lambda s: (s, 0, 0)),
                  pl.BlockSpec((None, nblk, h, hd), lambda s: (s, 0, 0, 0))],
        out_specs=pl.BlockSpec((None, SUBLANES, h), lambda s: (s, 0, 0)),
        out_shape=jax.ShapeDtypeStruct((nseq, SUBLANES, h), jnp.int32),
        compiler_params=_params(1),
        name="select_blocks",
    )(q_s, kmean)


def _moba_sample_kernel(pt_ref, idx_ref, slopes_ref, q_ref, kn_ref, vn_ref, *refs,
                        page, past_len, hd):
    n_pg = MOBA_TOP_K * (MOBA_BLOCK // page)
    k_refs, v_refs = refs[:n_pg], refs[n_pg:2 * n_pg]
    o_ref, kd_scr, vd_scr = refs[2 * n_pg:]
    s_id = pl.program_id(0)
    h = pl.program_id(1)
    slope = slopes_ref[h]
    scale = 1.0 / math.sqrt(hd)

    for c in range(SUBLANES):
        @pl.when(h % SUBLANES == c)
        def _():
            for t in range(n_pg):
                kd_scr[t] = k_refs[t][:, c, :]
                vd_scr[t] = v_refs[t][:, c, :]

    q = q_ref[...]
    q8 = jnp.broadcast_to(q, (SUBLANES, hd))
    lane = lax.broadcasted_iota(jnp.int32, (1, page), 1)
    logits = []
    for t in range(n_pg):
        r, half = divmod(t, MOBA_BLOCK // page)
        blk = idx_ref[(s_id * N_HEADS + h) * MOBA_TOP_K + r]
        s = lax.dot_general(q8, kd_scr[t], (((1,), (1,)), ((), ())), precision=HIGHEST,
                            preferred_element_type=F32)[0:1]
        dist = (past_len - blk * MOBA_BLOCK - half * page - lane).astype(F32)
        logits.append(s * scale - slope * dist)
    l_self = jnp.sum(q * kn_ref[...], axis=1, keepdims=True) * scale
    mx = l_self
    for l in logits:
        mx = jnp.maximum(mx, jnp.max(l, axis=1, keepdims=True))
    p_self = jnp.exp(l_self - mx)
    denom = p_self
    acc = p_self * vn_ref[...]
    for t in range(n_pg):
        p = jnp.exp(logits[t] - mx)
        denom = denom + jnp.sum(p, axis=1, keepdims=True)
        acc = acc + jnp.dot(jnp.broadcast_to(p, (SUBLANES, page)), vd_scr[t],
                            precision=HIGHEST, preferred_element_type=F32)[0:1]
    o_ref[...] = acc / denom


def _moba_sample(page_table, top_idx, slopes, q_s, k_s, v_s, cache_k, cache_v, past_len):
    _, page, h, hd = cache_k.shape
    nseq = q_s.shape[0]
    per_blk = MOBA_BLOCK // page
    n_pg = MOBA_TOP_K * per_blk

    def page_spec(t):
        r, half = divmod(t, per_blk)
        return pl.BlockSpec(
            (None, page, SUBLANES, hd),
            lambda s, hh, pt, idx: (
                pt[s, per_blk * idx[(s * N_HEADS + hh) * MOBA_TOP_K + r] + half], 0,
                hh // SUBLANES, 0))

    row_spec = pl.BlockSpec((None, None, 1, hd), lambda s, hh, pt, idx: (s, hh, 0, 0))
    grid_spec = pltpu.PrefetchScalarGridSpec(
        num_scalar_prefetch=2,
        grid=(nseq, h),
        in_specs=([pl.BlockSpec(memory_space=pltpu.SMEM), row_spec, row_spec, row_spec]
                  + [page_spec(t) for t in range(n_pg)]
                  + [page_spec(t) for t in range(n_pg)]),
        out_specs=row_spec,
        scratch_shapes=[pltpu.VMEM((n_pg, page, hd), F32), pltpu.VMEM((n_pg, page, hd), F32)],
    )
    return pl.pallas_call(
        functools.partial(_moba_sample_kernel, page=page, past_len=past_len, hd=hd),
        grid_spec=grid_spec,
        out_shape=jax.ShapeDtypeStruct((nseq, h, 1, hd), F32),
        compiler_params=_params(2),
        name="moba_sample",
    )(page_table, top_idx, slopes, q_s, k_s, v_s, *([cache_k] * n_pg), *([cache_v] * n_pg))


def _rg_gates(xc, wa_ref, ba_ref, wi_ref, bi_ref, lam_ref):
    xcb = xc.astype(BF16)
    r = jax.nn.sigmoid(
        jnp.dot(xcb, wa_ref[...].astype(BF16), preferred_element_type=F32) + ba_ref[...])
    i = jax.nn.sigmoid(
        jnp.dot(xcb, wi_ref[...].astype(BF16), preferred_element_type=F32) + bi_ref[...])
    neg_lam = -lam_ref[...]
    softplus = jnp.maximum(neg_lam, 0.0) + jnp.log1p(jnp.exp(-jnp.abs(neg_lam)))
    log_a = -RG_C * r * softplus
    a = jnp.exp(log_a)
    mult = jnp.sqrt(jnp.tanh(-log_a) * (a * a + 1.0))
    return a, mult * (i * xc)


def _rglru_kernel(xr_ref, yg_ref, xrs_ref, ygs_ref, cs_ref, h0_ref, cw_ref, cb_ref,
                  wa_ref, ba_ref, wi_ref, bi_ref, lam_ref,
                  o_ref, hl_ref, os_ref, hs_ref, xs_scr, a_scr, b_scr, h_scr, *, nb, seq):
    b = pl.program_id(1)
    cw = cw_ref[...]
    width = cw.shape[0]
    gate_refs = (wa_ref, ba_ref, wi_ref, bi_ref, lam_ref)

    @pl.when(b < nb)
    def _():
        x = xr_ref[...]
        xs_scr[0:SUBLANES, :] = jnp.zeros((SUBLANES, x.shape[1]), F32)
        xs_scr[SUBLANES:, :] = x
        xc = xs_scr[pl.ds(SUBLANES - (width - 1), seq), :] * cw[0:1]
        for j in range(1, width):
            xc = xc + xs_scr[pl.ds(SUBLANES - (width - 1) + j, seq), :] * cw[j:j + 1]
        xc = xc + cb_ref[...]
        a, bx = _rg_gates(xc, *gate_refs)
        a_scr[...] = a
        b_scr[...] = bx
        row = lax.broadcasted_iota(jnp.int32, (SUBLANES, x.shape[1]), 0)

        def body(g, h):
            off = pl.multiple_of(g * SUBLANES, SUBLANES)
            aa = a_scr[pl.ds(off, SUBLANES), :]
            bb = b_scr[pl.ds(off, SUBLANES), :]
            for d in (1, 2, 4):
                a_sh = jnp.where(row >= d, pltpu.roll(aa, d, 0), 1.0)
                b_sh = jnp.where(row >= d, pltpu.roll(bb, d, 0), 0.0)
                bb = aa * b_sh + bb
                aa = aa * a_sh
            hh = aa * h + bb
            h_scr[pl.ds(off, SUBLANES), :] = hh
            return hh[SUBLANES - 1:SUBLANES, :]

        h_last = lax.fori_loop(0, seq // SUBLANES, body, jnp.zeros((1, x.shape[1]), F32),
                               unroll=4)
        hl_ref[...] = h_last
        o_ref[...] = (h_scr[...] * yg_ref[...]).astype(o_ref.dtype)

    @pl.when(b == nb)
    def _():
        xc = cs_ref[0] * cw[0:1]
        for j in range(1, width - 1):
            xc = xc + cs_ref[j] * cw[j:j + 1]
        xc = xc + xrs_ref[...] * cw[width - 1:width] + cb_ref[...]
        a, bx = _rg_gates(xc, *gate_refs)
        hh = a * h0_ref[...] + bx
        hs_ref[...] = hh
        os_ref[...] = hh * ygs_ref[...]


def _rglru(xr, yg, xr_s, yg_s, conv_state, h0, conv_w, conv_b, w_a, b_a, w_i, b_i, lam, j,
           batch, seq):
    d = xr.shape[1]
    ms = xr_s.shape[0]
    nblk, cw = w_a.shape[1], w_a.shape[2]
    width = conv_w.shape[1]
    last = batch - 1
    seq_spec = pl.BlockSpec((seq, cw), lambda c, b: (jnp.minimum(b, last), c))
    smp_spec = pl.BlockSpec((ms, cw), lambda c, b: (0, c))
    vec_spec = pl.BlockSpec((None, 1, cw), lambda c, b: (j, 0, c))
    mat_spec = pl.BlockSpec((None, None, cw, cw), lambda c, b: (j, c, 0, 0))
    return pl.pallas_call(
        functools.partial(_rglru_kernel, nb=batch, seq=seq),
        grid=(nblk, batch + 1),
        in_specs=[
            seq_spec, seq_spec, smp_spec, smp_spec,
            pl.BlockSpec((width - 1, ms, cw), lambda c, b: (0, 0, c)),
            pl.BlockSpec((None, ms, cw), lambda c, b: (j, 0, c)),
            pl.BlockSpec((None, width, cw), lambda c, b: (j, 0, c)),
            vec_spec, mat_spec, vec_spec, mat_spec, vec_spec, vec_spec,
        ],
        out_specs=[
            seq_spec,
            pl.BlockSpec((None, 1, cw), lambda c, b: (jnp.minimum(b, last), 0, c)),
            smp_spec, smp_spec,
        ],
        out_shape=[
            jax.ShapeDtypeStruct((batch * seq, d), BF16),
            jax.ShapeDtypeStruct((batch, 1, d), F32),
            jax.ShapeDtypeStruct((ms, d), F32),
            jax.ShapeDtypeStruct((ms, d), F32),
        ],
        scratch_shapes=[pltpu.VMEM((seq + SUBLANES, cw), F32),
                        pltpu.VMEM((seq, cw), F32),
                        pltpu.VMEM((seq, cw), F32),
                        pltpu.VMEM((seq, cw), F32)],
        compiler_params=_params(2),
        name="rglru",
    )(xr, yg, xr_s, yg_s, conv_state, h0, conv_w, conv_b, w_a, b_a, w_i, b_i, lam)


def _gelu_erf(x):
    return 0.5 * x * (1.0 + lax.erf(x * math.sqrt(0.5)))


def _ffn_up_kernel(ap_ref, as_ref, wg_ref, wu_ref, s_ref, cw_ref, cb_ref,
                   hp_ref, hs_ref, tail_ref, gs_ref, wgb_ref, wub_ref, g_scr, *, nm, tiles_per_seq):
    m = pl.program_id(1)
    cw = cw_ref[...]
    width = cw.shape[0]
    tm = ap_ref.shape[0]

    @pl.when(m == 0)
    def _():
        wgb_ref[...] = wg_ref[...].astype(BF16)
        wub_ref[...] = wu_ref[...].astype(BF16)

    @pl.when(m < nm)
    def _():
        a = ap_ref[...]
        g = jnp.dot(a, wgb_ref[...], preferred_element_type=F32)
        u = jnp.dot(a, wub_ref[...], preferred_element_type=F32)

        @pl.when(m % tiles_per_seq == 0)
        def _():
            g_scr[0:SUBLANES, :] = jnp.zeros((SUBLANES, g.shape[1]), F32)

        @pl.when(m % tiles_per_seq != 0)
        def _():
            g_scr[0:SUBLANES, :] = g_scr[tm:tm + SUBLANES, :]

        g_scr[SUBLANES:, :] = g
        y = g_scr[pl.ds(SUBLANES - (width - 1), tm), :] * cw[0:1]
        for j in range(1, width - 1):
            y = y + g_scr[pl.ds(SUBLANES - (width - 1) + j, tm), :] * cw[j:j + 1]
        y = y + g * cw[width - 1:width] + cb_ref[...]
        hp_ref[...] = (_gelu_erf(y) * u).astype(hp_ref.dtype)

        @pl.when(m % tiles_per_seq == tiles_per_seq - 1)
        def _():
            tail_ref[...] = g[tm - SUBLANES:, :]

    @pl.when(m == nm)
    def _():
        a = as_ref[...].astype(BF16)
        g = jnp.dot(a, wgb_ref[...], preferred_element_type=F32)
        u = jnp.dot(a, wub_ref[...], preferred_element_type=F32)
        y = s_ref[0] * cw[0:1]
        for j in range(1, width - 1):
            y = y + s_ref[j] * cw[j:j + 1]
        y = y + g * cw[width - 1:width] + cb_ref[...]
        hs_ref[...] = _gelu_erf(y) * u
        gs_ref[...] = g


def _ffn_up(ap, a_s, w_gate, w_up, state, conv_w, conv_b, layer, batch, seq, tn):
    mp, k = ap.shape
    ms = a_s.shape[0]
    f = w_gate.shape[2]
    width = conv_w.shape[1]
    tm = MM_ROWS
    nm = mp // tm
    last = nm - 1
    tps = seq // tm
    w_spec = pl.BlockSpec((None, k, tn), lambda n, m: (layer, 0, n))
    return pl.pallas_call(
        functools.partial(_ffn_up_kernel, nm=nm, tiles_per_seq=tps),
        grid=(f // tn, nm + 1),
        in_specs=[
            pl.BlockSpec((tm, k), lambda n, m: (jnp.minimum(m, last), 0)),
            pl.BlockSpec((ms, k), lambda n, m: (0, 0)),
            w_spec, w_spec,
            pl.BlockSpec((width - 1, ms, tn), lambda n, m: (0, 0, n)),
            pl.BlockSpec((None, width, tn), lambda n, m: (layer, 0, n)),
            pl.BlockSpec((None, 1, tn), lambda n, m: (layer, 0, n)),
        ],
        out_specs=[
            pl.BlockSpec((tm, tn), lambda n, m: (jnp.minimum(m, last), n)),
            pl.BlockSpec((ms, tn), lambda n, m: (0, n)),
            pl.BlockSpec((None, SUBLANES, tn), lambda n, m: (jnp.minimum(m, last) // tps, 0, n)),
            pl.BlockSpec((ms, tn), lambda n, m: (0, n)),
        ],
        out_shape=[
            jax.ShapeDtypeStruct((mp, f), BF16),
            jax.ShapeDtypeStruct((ms, f), F32),
            jax.ShapeDtypeStruct((batch, SUBLANES, f), F32),
            jax.ShapeDtypeStruct((ms, f), F32),
        ],
        scratch_shapes=[pltpu.VMEM((k, tn), BF16), pltpu.VMEM((k, tn), BF16),
                        pltpu.VMEM((tm + SUBLANES, tn), F32)],
        compiler_params=_params(2),
        name="ffn_up",
    )(ap, a_s, w_gate, w_up, state, conv_w, conv_b)


def kernel(x_prompt, x_sample, cache_k_l0, cache_v_l0, cache_k_l2, cache_v_l2, page_table, state_rglru_h, state_rglru_conv, state_ffn_conv, norm_mix, norm_ffn, norm_final, attn_w_qkv, attn_w_o, rg_w_gate, rg_w_x, rg_conv_w, rg_conv_b, rg_w_a, rg_b_a, rg_w_i, rg_b_i, rg_lambda, rg_w_out, ffn_w_gate, ffn_w_up, ffn_conv_w, ffn_conv_b, ffn_w_down):
    batch, seq, d = x_prompt.shape
    nseq, dec_seq, _ = x_sample.shape
    assert dec_seq == 1 and seq % MM_ROWS == 0 and seq % MOBA_BLOCK == 0
    depth = norm_mix.shape[0]
    hd = d // N_HEADS
    page = cache_k_l0.shape[1]
    past_len = page_table.shape[1] * page
    assert past_len % MOBA_BLOCK == 0
    d_ff = ffn_w_gate.shape[2]
    mp = batch * seq

    xp = x_prompt.reshape(mp, d)
    xs = x_sample.reshape(nseq, d)
    slopes = jnp.exp2(-8.0 * jnp.arange(1, N_HEADS + 1, dtype=F32) / N_HEADS)
    caches = [(cache_k_l0, cache_v_l0), (cache_k_l2, cache_v_l2)]
    norm_mix3 = norm_mix.reshape(depth, 1, d)
    norm_ffn3 = norm_ffn.reshape(depth, 1, d)
    rg_conv_b3 = rg_conv_b.reshape(-1, 1, d)
    rg_b_a3 = rg_b_a.reshape(-1, 1, d)
    rg_b_i3 = rg_b_i.reshape(-1, 1, d)
    rg_lam3 = rg_lambda.reshape(-1, 1, d)
    ffn_conv_b3 = ffn_conv_b.reshape(depth, 1, d_ff)

    att_out, rg_h_p, rg_h_s, rg_c_p, rg_c_s, ffn_c_p, ffn_c_s = [], [], [], [], [], [], []
    for layer in range(depth):
        j = layer // 2
        hp, hs = _rmsnorm(xp, xs, norm_mix3, layer, BF16)
        if layer % 2 == 0:
            ck, cv = caches[j]
            qp, qs = _matmul(hp, hs, attn_w_qkv, j, col_off=0, n_out=d, tn=1024)
            kp, ks = _matmul(hp, hs, attn_w_qkv, j, col_off=d, n_out=d, tn=1024)
            vp, vs = _matmul(hp, hs, attn_w_qkv, j, col_off=2 * d, n_out=d, tn=1024)
            op = _moba_prompt(slopes, qp, kp, vp, batch, seq)
            kmean = _page_kmean(ck, page_table)
            picks = _select_blocks(qs.reshape(nseq, N_HEADS, hd), kmean)
            top_idx = picks[:, :MOBA_TOP_K].transpose(0, 2, 1).reshape(-1)
            qs4, ks4, vs4 = (t.reshape(nseq, N_HEADS, 1, hd) for t in (qs, ks, vs))
            o_s = _moba_sample(page_table, top_idx, slopes, qs4, ks4, vs4, ck, cv, past_len)
            xp, xs = _matmul(op, o_s.reshape(nseq, d), attn_w_o, j, n_out=d, tn=1024,
                             res=(xp, xs))
            att_out += [kp.reshape(batch, seq, N_HEADS, hd), vp.reshape(batch, seq, N_HEADS, hd),
                        ks.reshape(nseq, 1, N_HEADS, hd), vs.reshape(nseq, 1, N_HEADS, hd)]
        else:
            ygp, ygs = _matmul(hp, hs, rg_w_gate, j, n_out=d, tn=1024, epilogue="gelu_tanh")
            xrp, xrs = _matmul(hp, hs, rg_w_x, j, n_out=d, tn=1024)
            conv_state = state_rglru_conv[j].transpose(1, 0, 2)
            gp, hl_p, gs, hl_s = _rglru(xrp, ygp, xrs, ygs, conv_state, state_rglru_h,
                                        rg_conv_w, rg_conv_b3, rg_w_a, rg_b_a3, rg_w_i, rg_b_i3,
                                        rg_lam3, j, batch, seq)
            xp, xs = _matmul(gp, gs, rg_w_out, j, n_out=d, tn=1024, res=(xp, xs))
            keep = rg_conv_w.shape[1] - 1
            rg_h_p.append(hl_p.reshape(batch, d))
            rg_h_s.append(hl_s)
            rg_c_p.append(xrp.reshape(batch, seq, d)[:, seq - keep:])
            rg_c_s.append(jnp.concatenate([state_rglru_conv[j][:, 1:], xrs[:, None, :]], axis=1))
        hp, hs = _rmsnorm(xp, xs, norm_ffn3, layer, BF16)
        f_state = state_ffn_conv[layer].transpose(1, 0, 2)
        up_p, up_s, tail, g_s = _ffn_up(hp, hs, ffn_w_gate, ffn_w_up, f_state, ffn_conv_w,
                                        ffn_conv_b3, layer, batch, seq, tn=512)
        xp, xs = _matmul(up_p, up_s, ffn_w_down, layer, n_out=d, tn=512, res=(xp, xs),
                         w_buffers=1)
        keep = ffn_conv_w.shape[1] - 1
        ffn_c_p.append(tail[:, SUBLANES - keep:])
        ffn_c_s.append(jnp.concatenate([state_ffn_conv[layer][:, 1:], g_s[:, None, :]], axis=1))

    yp, ys = _rmsnorm(xp, xs, norm_final.reshape(1, 1, d), 0, F32)
    return (yp.reshape(batch, seq, d), ys.reshape(nseq, 1, d), *att_out,
            jnp.stack(rg_h_p), jnp.stack(rg_h_s), jnp.stack(rg_c_p), jnp.stack(rg_c_s),
            jnp.stack(ffn_c_p), jnp.stack(ffn_c_s))
```

```python
import functools
import math

import jax
import jax.numpy as jnp
from jax import lax
from jax.experimental import pallas as pl
from jax.experimental.pallas import tpu as pltpu

F32 = jnp.float32
BF16 = jnp.bfloat16
HIGHEST = lax.Precision.HIGHEST

N_HEADS = 16
MOBA_BLOCK = 256
BLOCK_SHIFT = MOBA_BLOCK.bit_length() - 1
MOBA_TOP_K = 3
RG_C = 8.0
NORM_EPS = 1e-6
MASK_BIG = 2.0 ** 100
LOG2_E = math.log2(math.e)

V7X_VMEM_BYTES = 64 * 1024 * 1024
VMEM_LIMIT = V7X_VMEM_BYTES - 8 * 1024 * 1024
LANES = 128
SUBLANES = 8

NORM_ROWS = 512
MM_ROWS = 1024


def _params(n_axes):
    return pltpu.CompilerParams(
        dimension_semantics=("arbitrary",) * n_axes, vmem_limit_bytes=VMEM_LIMIT)


def _norm_kernel(xp_ref, xs_ref, g_ref, op_ref, os_ref, *, nm):
    m = pl.program_id(0)
    g = g_ref[...]

    def nrm(x):
        y = x * lax.rsqrt(jnp.mean(x * x, axis=-1, keepdims=True) + NORM_EPS)
        return y * g

    @pl.when(m < nm)
    def _():
        op_ref[...] = nrm(xp_ref[...]).astype(op_ref.dtype)

    @pl.when(m == nm)
    def _():
        os_ref[...] = nrm(xs_ref[...]).astype(os_ref.dtype)


def _rmsnorm(xp, xs, gains, layer, out_dtype):
    mp, d = xp.shape
    ms = xs.shape[0]
    nm = mp // NORM_ROWS
    last = nm - 1
    return pl.pallas_call(
        functools.partial(_norm_kernel, nm=nm),
        grid=(nm + 1,),
        in_specs=[
            pl.BlockSpec((NORM_ROWS, d), lambda m: (jnp.minimum(m, last), 0)),
            pl.BlockSpec((ms, d), lambda m: (0, 0)),
            pl.BlockSpec((None, 1, d), lambda m: (layer, 0, 0)),
        ],
        out_specs=[
            pl.BlockSpec((NORM_ROWS, d), lambda m: (jnp.minimum(m, last), 0)),
            pl.BlockSpec((ms, d), lambda m: (0, 0)),
        ],
        out_shape=[jax.ShapeDtypeStruct((mp, d), out_dtype),
                   jax.ShapeDtypeStruct((ms, d), F32)],
        compiler_params=_params(1),
        name="rmsnorm",
    )(xp, xs, gains)


def _gelu_tanh(x):
    return jax.nn.gelu(x, approximate=True)


def _mm_kernel(*refs, nm, epilogue, has_res):
    if has_res:
        ap_ref, as_ref, w_ref, rp_ref, rs_ref, op_ref, os_ref, wb_ref = refs
    else:
        ap_ref, as_ref, w_ref, op_ref, os_ref, wb_ref = refs
        rp_ref = rs_ref = None
    m = pl.program_id(1)

    @pl.when(m == 0)
    def _():
        wb_ref[...] = w_ref[...].astype(BF16)

    def finish(acc, r_ref, o_ref):
        if epilogue == "gelu_tanh":
            acc = _gelu_tanh(acc)
        if r_ref is not None:
            acc = r_ref[...] + acc
        o_ref[...] = acc.astype(o_ref.dtype)

    @pl.when(m < nm)
    def _():
        acc = jnp.dot(ap_ref[...], wb_ref[...], preferred_element_type=F32)
        finish(acc, rp_ref, op_ref)

    @pl.when(m == nm)
    def _():
        acc = jnp.dot(as_ref[...].astype(BF16), wb_ref[...], preferred_element_type=F32)
        finish(acc, rs_ref, os_ref)


def _matmul(ap, a_s, w, layer, *, col_off=0, n_out, tn, epilogue=None, res=None,
            out_dtype=F32, w_buffers=2):
    mp, k = ap.shape
    ms = a_s.shape[0]
    tm = MM_ROWS
    nm = mp // tm
    last = nm - 1
    nn = n_out // tn
    off = col_off // tn
    in_specs = [
        pl.BlockSpec((tm, k), lambda n, m: (jnp.minimum(m, last), 0)),
        pl.BlockSpec((ms, k), lambda n, m: (0, 0)),
        pl.BlockSpec((None, k, tn), lambda n, m: (layer, 0, off + n),
                     pipeline_mode=pl.Buffered(w_buffers)),
    ]
    args = [ap, a_s, w]
    if res is not None:
        in_specs += [
            pl.BlockSpec((tm, tn), lambda n, m: (jnp.minimum(m, last), n)),
            pl.BlockSpec((ms, tn), lambda n, m: (0, n)),
        ]
        args += list(res)
    return pl.pallas_call(
        functools.partial(_mm_kernel, nm=nm, epilogue=epilogue, has_res=res is not None),
        grid=(nn, nm + 1),
        in_specs=in_specs,
        out_specs=[
            pl.BlockSpec((tm, tn), lambda n, m: (jnp.minimum(m, last), n)),
            pl.BlockSpec((ms, tn), lambda n, m: (0, n)),
        ],
        out_shape=[jax.ShapeDtypeStruct((mp, n_out), out_dtype),
                   jax.ShapeDtypeStruct((ms, n_out), F32)],
        scratch_shapes=[pltpu.VMEM((k, tn), BF16)],
        compiler_params=_params(2),
        name="matmul",
    )(*args)


def _moba_prompt_kernel(pt_ref, slopes_ref, q_ref, k_ref, v_ref, *refs, seq, hd, n_side,
                        pages_per_blk, steps_per_seq):
    page_refs = refs[:n_side]
    o_ref, km_ref, qa_scr, ka_scr, vb_scr, t_scr = refs[n_side:]
    nblk = seq // MOBA_BLOCK
    slope = slopes_ref[pl.program_id(1)]
    scale = 1.0 / math.sqrt(hd)

    step = pl.program_id(0) * pl.num_programs(1) + pl.program_id(1)
    blks = n_side // pages_per_blk
    rows = pages_per_blk * page_refs[0].shape[0]
    for b in range(blks):
        acc = jnp.sum(page_refs[b * pages_per_blk][...], axis=0)
        for p in range(1, pages_per_blk):
            acc = acc + jnp.sum(page_refs[b * pages_per_blk + p][...], axis=0)
        km_ref[(step % steps_per_seq) * blks + b] = acc * (1.0 / rows)

    q = q_ref[...]
    k = k_ref[...]

    kmean = jnp.mean(k.reshape(nblk, MOBA_BLOCK, hd), axis=1)
    gate = lax.dot_general(kmean, q, (((1,), (1,)), ((), ())), precision=HIGHEST,
                           preferred_element_type=F32)
    n_idx = lax.broadcasted_iota(jnp.int32, (nblk, seq), 0)
    own = lax.broadcasted_iota(jnp.int32, (nblk, seq), 1) >> BLOCK_SHIFT
    g = jnp.where(n_idx < own, gate, -jnp.inf)
    sel = n_idx == own
    for _ in range(MOBA_TOP_K):
        mx = jnp.max(g, axis=0, keepdims=True)
        first = jnp.min(jnp.where(g == mx, n_idx, nblk), axis=0, keepdims=True)
        pick = (n_idx == first) & (mx > -jnp.inf)
        sel = sel | pick
        g = jnp.where(pick, -jnp.inf, g)

    unsel_pad = jnp.concatenate(
        [jnp.where(sel, 0.0, 1.0), jnp.zeros((LANES - nblk, seq), F32)], axis=0)
    qa_scr[:, 0:hd] = q.astype(BF16)
    qa_scr[:, hd:] = unsel_pad.T.astype(BF16)
    ka_scr[:, 0:hd] = k.astype(BF16)
    ka_scr[:, hd:] = jnp.where(
        lax.broadcasted_iota(jnp.int32, (seq, LANES), 0) >> BLOCK_SHIFT
        == lax.broadcasted_iota(jnp.int32, (seq, LANES), 1), -MASK_BIG, 0.0).astype(BF16)
    vb_scr[...] = v_ref[...].astype(BF16)
    dist = (lax.broadcasted_iota(jnp.int32, (MOBA_BLOCK, seq), 0)
            - lax.broadcasted_iota(jnp.int32, (MOBA_BLOCK, seq), 1)
            + (seq - MOBA_BLOCK)).astype(F32)
    t_scr[...] = jnp.where(dist >= 0.0, (slope * LOG2_E) * dist, MASK_BIG)

    for i in range(nblk):
        lo, hi = i * MOBA_BLOCK, (i + 1) * MOBA_BLOCK
        s = lax.dot_general(qa_scr[lo:hi, :], ka_scr[0:hi, :], (((1,), (1,)), ((), ())),
                            preferred_element_type=F32)
        logit = s * (scale * LOG2_E) - t_scr[:, seq - hi:]
        mx = jnp.max(logit, axis=1, keepdims=True)
        p = jnp.exp2(logit - mx)
        denom = jnp.sum(p, axis=1, keepdims=True)
        o = jnp.dot(p.astype(BF16), vb_scr[0:hi, :], preferred_element_type=F32) / denom
        o_ref[lo:hi, :] = o.astype(o_ref.dtype)


def _moba_prompt(slopes, q, k, v, batch, seq, cache_k, page_table):
    d = q.shape[1]
    hd = d // N_HEADS
    _, page, h, _ = cache_k.shape
    nseq, n_pages = page_table.shape
    per_blk = MOBA_BLOCK // page
    steps = batch * N_HEADS
    n_side = (nseq * n_pages) // steps
    assert MOBA_BLOCK % page == 0 and n_side * steps == nseq * n_pages
    assert n_pages % n_side == 0 and n_side % per_blk == 0
    sps = n_pages // n_side

    def page_spec(t):
        def index(b, hh, pt):
            step = b * N_HEADS + hh
            return (pt[step // sps, (step % sps) * n_side + t], 0, 0, 0)
        return pl.BlockSpec((None, page, h, hd), index)

    spec = pl.BlockSpec((seq, hd), lambda b, hh, pt: (b, hh))
    grid_spec = pltpu.PrefetchScalarGridSpec(
        num_scalar_prefetch=1,
        grid=(batch, N_HEADS),
        in_specs=([pl.BlockSpec(memory_space=pltpu.SMEM), spec, spec, spec]
                  + [page_spec(t) for t in range(n_side)]),
        out_specs=[spec,
                   pl.BlockSpec((None, n_pages // per_blk, h, hd),
                                lambda b, hh, pt: ((b * N_HEADS + hh) // sps, 0, 0, 0))],
        scratch_shapes=[pltpu.VMEM((seq, hd + LANES), BF16), pltpu.VMEM((seq, hd + LANES), BF16),
                        pltpu.VMEM((seq, hd), BF16), pltpu.VMEM((MOBA_BLOCK, seq), F32)],
    )
    return pl.pallas_call(
        functools.partial(_moba_prompt_kernel, seq=seq, hd=hd, n_side=n_side,
                          pages_per_blk=per_blk, steps_per_seq=sps),
        grid_spec=grid_spec,
        out_shape=[jax.ShapeDtypeStruct((batch * seq, d), BF16),
                   jax.ShapeDtypeStruct((nseq, n_pages // per_blk, h, hd), F32)],
        compiler_params=_params(2),
        name="moba_prompt",
    )(page_table, slopes, q, k, v, *([cache_k] * n_side))


def _select_kernel(q_ref, km_ref, o_ref):
    nblk, h, _ = km_ref.shape
    g = jnp.sum(km_ref[...] * q_ref[...][None], axis=-1)
    n_idx = lax.broadcasted_iota(jnp.int32, (nblk, h), 0)
    rows = []
    for _ in range(MOBA_TOP_K):
        mx = jnp.max(g, axis=0, keepdims=True)
        first = jnp.min(jnp.where(g == mx, n_idx, nblk), axis=0, keepdims=True)
        rows.append(first)
        g = jnp.where(n_idx == first, -jnp.inf, g)
    rows.append(jnp.zeros((SUBLANES - MOBA_TOP_K, h), jnp.int32))
    o_ref[...] = jnp.concatenate(rows, axis=0)


def _select_blocks(q_s, kmean):
    nseq, nblk, h, hd = kmean.shape
    return pl.pallas_call(
        _select_kernel,
        grid=(nseq,),
        in_specs=[pl.BlockSpec((None, h, hd), lambda s: (s, 0, 0)),
                  pl.BlockSpec((None, nblk, h, hd), lambda s: (s, 0, 0, 0))],
        out_specs=pl.BlockSpec((None, SUBLANES, h), lambda s: (s, 0, 0)),
        out_shape=jax.ShapeDtypeStruct((nseq, SUBLANES, h), jnp.int32),
        compiler_params=_params(1),
        name="select_blocks",
    )(q_s, kmean)


def _moba_sample_kernel(pt_ref, idx_ref, slopes_ref, q_ref, kn_ref, vn_ref, kc_hbm, vc_hbm,
                        o_ref, kd_scr, vd_scr, sem, *, page, past_len, hd):
    per_blk = MOBA_BLOCK // page
    n_pg = MOBA_TOP_K * per_blk
    n_heads = q_ref.shape[0]
    s_id = pl.program_id(0)
    scale = 1.0 / math.sqrt(hd)

    def block_of(h, r):
        return idx_ref[(s_id * n_heads + h) * MOBA_TOP_K + r]

    def head_copies(h):
        out = []
        for t in range(n_pg):
            r, half = divmod(t, per_blk)
            pg = pt_ref[s_id, per_blk * block_of(h, r) + half]
            out.append(pltpu.make_async_copy(kc_hbm.at[pg, :, h, :], kd_scr.at[h, t], sem.at[h]))
            out.append(pltpu.make_async_copy(vc_hbm.at[pg, :, h, :], vd_scr.at[h, t], sem.at[h]))
        return out

    def start_head(h, carry):
        for cp in head_copies(h):
            cp.start()
        return carry

    lax.fori_loop(0, n_heads, start_head, 0)
    row = lax.broadcasted_iota(jnp.int32, (page, 1), 0)

    def attend_head(h, carry):
        for cp in head_copies(h):
            cp.wait()
        slope = slopes_ref[h]
        q = q_ref[pl.ds(h, 1), :]
        logits = []
        for t in range(n_pg):
            r, half = divmod(t, per_blk)
            s = jnp.sum(kd_scr[h, t] * q, axis=1, keepdims=True)
            dist = (past_len - block_of(h, r) * MOBA_BLOCK - half * page - row).astype(F32)
            logits.append(s * scale - slope * dist)
        l_self = jnp.sum(q * kn_ref[pl.ds(h, 1), :], axis=1, keepdims=True) * scale
        mx = l_self
        for l in logits:
            mx = jnp.maximum(mx, jnp.max(l, axis=0, keepdims=True))
        p_self = jnp.exp(l_self - mx)
        denom = p_self
        acc = p_self * vn_ref[pl.ds(h, 1), :]
        for t in range(n_pg):
            p = jnp.exp(logits[t] - mx)
            denom = denom + jnp.sum(p, axis=0, keepdims=True)
            acc = acc + jnp.sum(p * vd_scr[h, t], axis=0, keepdims=True)
        o_ref[pl.ds(h, 1), :] = acc / denom
        return carry

    lax.fori_loop(0, n_heads, attend_head, 0)


def _moba_sample(page_table, top_idx, slopes, q_s, k_s, v_s, cache_k, cache_v, past_len):
    _, page, h, hd = cache_k.shape
    nseq = q_s.shape[0]
    n_pg = MOBA_TOP_K * (MOBA_BLOCK // page)
    seq_spec = pl.BlockSpec((None, h, hd), lambda s, pt, idx: (s, 0, 0))
    grid_spec = pltpu.PrefetchScalarGridSpec(
        num_scalar_prefetch=2,
        grid=(nseq,),
        in_specs=[pl.BlockSpec(memory_space=pltpu.SMEM), seq_spec, seq_spec, seq_spec,
                  pl.BlockSpec(memory_space=pl.ANY), pl.BlockSpec(memory_space=pl.ANY)],
        out_specs=seq_spec,
        scratch_shapes=[pltpu.VMEM((h, n_pg, page, hd), F32), pltpu.VMEM((h, n_pg, page, hd), F32),
                        pltpu.SemaphoreType.DMA((h,))],
    )
    return pl.pallas_call(
        functools.partial(_moba_sample_kernel, page=page, past_len=past_len, hd=hd),
        grid_spec=grid_spec,
        out_shape=jax.ShapeDtypeStruct((nseq, h, hd), F32),
        compiler_params=_params(1),
        name="moba_sample",
    )(page_table, top_idx, slopes, q_s, k_s, v_s, cache_k, cache_v)


def _rg_gates(xc, wa_ref, ba_ref, wi_ref, bi_ref, lam_ref):
    xcb = xc.astype(BF16)
    r = jax.nn.sigmoid(
        jnp.dot(xcb, wa_ref[...].astype(BF16), preferred_element_type=F32) + ba_ref[...])
    i = jax.nn.sigmoid(
        jnp.dot(xcb, wi_ref[...].astype(BF16), preferred_element_type=F32) + bi_ref[...])
    neg_lam = -lam_ref[...]
    softplus = jnp.maximum(neg_lam, 0.0) + jnp.log1p(jnp.exp(-jnp.abs(neg_lam)))
    log_a = -RG_C * r * softplus
    a = jnp.exp(log_a)
    mult = jnp.sqrt(jnp.tanh(-log_a) * (a * a + 1.0))
    return a, mult * (i * xc)


def _rglru_kernel(xr_ref, yg_ref, xrs_ref, ygs_ref, cs_ref, h0_ref, cw_ref, cb_ref,
                  wa_ref, ba_ref, wi_ref, bi_ref, lam_ref,
                  o_ref, hl_ref, os_ref, hs_ref, xs_scr, a_scr, b_scr, h_scr, *, nb, seq):
    b = pl.program_id(1)
    cw = cw_ref[...]
    width = cw.shape[0]
    gate_refs = (wa_ref, ba_ref, wi_ref, bi_ref, lam_ref)

    @pl.when(b < nb)
    def _():
        x = xr_ref[...]
        xs_scr[0:SUBLANES, :] = jnp.zeros((SUBLANES, x.shape[1]), F32)
        xs_scr[SUBLANES:, :] = x
        xc = xs_scr[pl.ds(SUBLANES - (width - 1), seq), :] * cw[0:1]
        for j in range(1, width):
            xc = xc + xs_scr[pl.ds(SUBLANES - (width - 1) + j, seq), :] * cw[j:j + 1]
        xc = xc + cb_ref[...]
        a, bx = _rg_gates(xc, *gate_refs)
        a_scr[...] = a
        b_scr[...] = bx
        row = lax.broadcasted_iota(jnp.int32, (SUBLANES, x.shape[1]), 0)

        def body(g, h):
            off = pl.multiple_of(g * SUBLANES, SUBLANES)
            aa = a_scr[pl.ds(off, SUBLANES), :]
            bb = b_scr[pl.ds(off, SUBLANES), :]
            for d in (1, 2, 4):
                a_sh = jnp.where(row >= d, pltpu.roll(aa, d, 0), 1.0)
                b_sh = jnp.where(row >= d, pltpu.roll(bb, d, 0), 0.0)
                bb = aa * b_sh + bb
                aa = aa * a_sh
            hh = aa * h + bb
            h_scr[pl.ds(off, SUBLANES), :] = hh
            return hh[SUBLANES - 1:SUBLANES, :]

        h_last = lax.fori_loop(0, seq // SUBLANES, body, jnp.zeros((1, x.shape[1]), F32),
                               unroll=4)
        hl_ref[...] = h_last
        o_ref[...] = (h_scr[...] * yg_ref[...]).astype(o_ref.dtype)

    @pl.when(b == nb)
    def _():
        xc = cs_ref[0] * cw[0:1]
        for j in range(1, width - 1):
            xc = xc + cs_ref[j] * cw[j:j + 1]
        xc = xc + xrs_ref[...] * cw[width - 1:width] + cb_ref[...]
        a, bx = _rg_gates(xc, *gate_refs)
        hh = a * h0_ref[...] + bx
        hs_ref[...] = hh
        os_ref[...] = hh * ygs_ref[...]


def _rglru(xr, yg, xr_s, yg_s, conv_state, h0, conv_w, conv_b, w_a, b_a, w_i, b_i, lam, j,
           batch, seq):
    d = xr.shape[1]
    ms = xr_s.shape[0]
    nblk, cw = w_a.shape[1], w_a.shape[2]
    width = conv_w.shape[1]
    last = batch - 1
    seq_spec = pl.BlockSpec((seq, cw), lambda c, b: (jnp.minimum(b, last), c))
    smp_spec = pl.BlockSpec((ms, cw), lambda c, b: (0, c))
    vec_spec = pl.BlockSpec((None, 1, cw), lambda c, b: (j, 0, c))
    mat_spec = pl.BlockSpec((None, None, cw, cw), lambda c, b: (j, c, 0, 0))
    return pl.pallas_call(
        functools.partial(_rglru_kernel, nb=batch, seq=seq),
        grid=(nblk, batch + 1),
        in_specs=[
            seq_spec, seq_spec, smp_spec, smp_spec,
            pl.BlockSpec((width - 1, ms, cw), lambda c, b: (0, 0, c)),
            pl.BlockSpec((None, ms, cw), lambda c, b: (j, 0, c)),
            pl.BlockSpec((None, width, cw), lambda c, b: (j, 0, c)),
            vec_spec, mat_spec, vec_spec, mat_spec, vec_spec, vec_spec,
        ],
        out_specs=[
            seq_spec,
            pl.BlockSpec((None, 1, cw), lambda c, b: (jnp.minimum(b, last), 0, c)),
            smp_spec, smp_spec,
        ],
        out_shape=[
            jax.ShapeDtypeStruct((batch * seq, d), BF16),
            jax.ShapeDtypeStruct((batch, 1, d), F32),
            jax.ShapeDtypeStruct((ms, d), F32),
            jax.ShapeDtypeStruct((ms, d), F32),
        ],
        scratch_shapes=[pltpu.VMEM((seq + SUBLANES, cw), F32),
                        pltpu.VMEM((seq, cw), F32),
                        pltpu.VMEM((seq, cw), F32),
                        pltpu.VMEM((seq, cw), F32)],
        compiler_params=_params(2),
        name="rglru",
    )(xr, yg, xr_s, yg_s, conv_state, h0, conv_w, conv_b, w_a, b_a, w_i, b_i, lam)


def _gelu_erf(x):
    return 0.5 * x * (1.0 + lax.erf(x * math.sqrt(0.5)))


def _ffn_up_kernel(ap_ref, as_ref, wg_ref, wu_ref, s_ref, cw_ref, cb_ref,
                   hp_ref, hs_ref, tail_ref, gs_ref, wgb_ref, wub_ref, g_scr, *, nm, tiles_per_seq):
    m = pl.program_id(1)
    cw = cw_ref[...]
    width = cw.shape[0]
    tm = ap_ref.shape[0]

    @pl.when(m == 0)
    def _():
        wgb_ref[...] = wg_ref[...].astype(BF16)
        wub_ref[...] = wu_ref[...].astype(BF16)
        g_scr[tm:, :] = jnp.zeros((SUBLANES, g_scr.shape[1]), F32)

    @pl.when(m < nm)
    def _():
        a = ap_ref[...]
        g = jnp.dot(a, wgb_ref[...], preferred_element_type=F32)
        g_scr[0:SUBLANES, :] = jnp.where(m % tiles_per_seq == 0, 0.0, g_scr[tm:, :])
        g_scr[SUBLANES:, :] = g
        y = g_scr[pl.ds(SUBLANES - (width - 1), tm), :] * cw[0:1]
        for j in range(1, width - 1):
            y = y + g_scr[pl.ds(SUBLANES - (width - 1) + j, tm), :] * cw[j:j + 1]
        y = y + g * cw[width - 1:width] + cb_ref[...]
        act = _gelu_erf(y)
        u = jnp.dot(a, wub_ref[...], preferred_element_type=F32)
        hp_ref[...] = (act * u).astype(hp_ref.dtype)
        tail_ref[...] = g[tm - SUBLANES:, :]

    @pl.when(m == nm)
    def _():
        a = as_ref[...].astype(BF16)
        g = jnp.dot(a, wgb_ref[...], preferred_element_type=F32)
        u = jnp.dot(a, wub_ref[...], preferred_element_type=F32)
        y = s_ref[0] * cw[0:1]
        for j in range(1, width - 1):
            y = y + s_ref[j] * cw[j:j + 1]
        y = y + g * cw[width - 1:width] + cb_ref[...]
        hs_ref[...] = _gelu_erf(y) * u
        gs_ref[...] = g


def _ffn_up(ap, a_s, w_gate, w_up, state, conv_w, conv_b, layer, batch, seq, tn):
    mp, k = ap.shape
    ms = a_s.shape[0]
    f = w_gate.shape[2]
    width = conv_w.shape[1]
    tm = MM_ROWS
    nm = mp // tm
    last = nm - 1
    tps = seq // tm
    w_spec = pl.BlockSpec((None, k, tn), lambda n, m: (layer, 0, n))
    return pl.pallas_call(
        functools.partial(_ffn_up_kernel, nm=nm, tiles_per_seq=tps),
        grid=(f // tn, nm + 1),
        in_specs=[
            pl.BlockSpec((tm, k), lambda n, m: (jnp.minimum(m, last), 0)),
            pl.BlockSpec((ms, k), lambda n, m: (0, 0)),
            w_spec, w_spec,
            pl.BlockSpec((width - 1, ms, tn), lambda n, m: (0, 0, n)),
            pl.BlockSpec((None, width, tn), lambda n, m: (layer, 0, n)),
            pl.BlockSpec((None, 1, tn), lambda n, m: (layer, 0, n)),
        ],
        out_specs=[
            pl.BlockSpec((tm, tn), lambda n, m: (jnp.minimum(m, last), n)),
            pl.BlockSpec((ms, tn), lambda n, m: (0, n)),
            pl.BlockSpec((None, SUBLANES, tn), lambda n, m: (jnp.minimum(m, last) // tps, 0, n)),
            pl.BlockSpec((ms, tn), lambda n, m: (0, n)),
        ],
        out_shape=[
            jax.ShapeDtypeStruct((mp, f), BF16),
            jax.ShapeDtypeStruct((ms, f), F32),
            jax.ShapeDtypeStruct((batch, SUBLANES, f), F32),
            jax.ShapeDtypeStruct((ms, f), F32),
        ],
        scratch_shapes=[pltpu.VMEM((k, tn), BF16), pltpu.VMEM((k, tn), BF16),
                        pltpu.VMEM((tm + SUBLANES, tn), F32)],
        compiler_params=_params(2),
        name="ffn_up",
    )(ap, a_s, w_gate, w_up, state, conv_w, conv_b)


def kernel(x_prompt, x_sample, cache_k_l0, cache_v_l0, cache_k_l2, cache_v_l2, page_table, state_rglru_h, state_rglru_conv, state_ffn_conv, norm_mix, norm_ffn, norm_final, attn_w_qkv, attn_w_o, rg_w_gate, rg_w_x, rg_conv_w, rg_conv_b, rg_w_a, rg_b_a, rg_w_i, rg_b_i, rg_lambda, rg_w_out, ffn_w_gate, ffn_w_up, ffn_conv_w, ffn_conv_b, ffn_w_down):
    batch, seq, d = x_prompt.shape
    nseq, dec_seq, _ = x_sample.shape
    assert dec_seq == 1 and seq % MM_ROWS == 0 and seq % MOBA_BLOCK == 0
    depth = norm_mix.shape[0]
    hd = d // N_HEADS
    page = cache_k_l0.shape[1]
    past_len = page_table.shape[1] * page
    assert past_len % MOBA_BLOCK == 0
    d_ff = ffn_w_gate.shape[2]
    mp = batch * seq

    xp = x_prompt.reshape(mp, d)
    xs = x_sample.reshape(nseq, d)
    slopes = jnp.exp2(-8.0 * jnp.arange(1, N_HEADS + 1, dtype=F32) / N_HEADS)
    caches = [(cache_k_l0, cache_v_l0), (cache_k_l2, cache_v_l2)]
    norm_mix3 = norm_mix.reshape(depth, 1, d)
    norm_ffn3 = norm_ffn.reshape(depth, 1, d)
    rg_conv_b3 = rg_conv_b.reshape(-1, 1, d)
    rg_b_a3 = rg_b_a.reshape(-1, 1, d)
    rg_b_i3 = rg_b_i.reshape(-1, 1, d)
    rg_lam3 = rg_lambda.reshape(-1, 1, d)
    ffn_conv_b3 = ffn_conv_b.reshape(depth, 1, d_ff)

    att_out, rg_h_p, rg_h_s, rg_c_p, rg_c_s, ffn_c_p, ffn_c_s = [], [], [], [], [], [], []
    for layer in range(depth):
        j = layer // 2
        hp, hs = _rmsnorm(xp, xs, norm_mix3, layer, BF16)
        if layer % 2 == 0:
            ck, cv = caches[j]
            qp, qs = _matmul(hp, hs, attn_w_qkv, j, col_off=0, n_out=d, tn=1024)
            kp, ks = _matmul(hp, hs, attn_w_qkv, j, col_off=d, n_out=d, tn=1024)
            vp, vs = _matmul(hp, hs, attn_w_qkv, j, col_off=2 * d, n_out=d, tn=1024)
            op, kmean = _moba_prompt(slopes, qp, kp, vp, batch, seq, ck, page_table)
            qs3, ks3, vs3 = (t.reshape(nseq, N_HEADS, hd) for t in (qs, ks, vs))
            picks = _select_blocks(qs3, kmean)
            top_idx = picks[:, :MOBA_TOP_K].transpose(0, 2, 1).reshape(-1)
            o_s = _moba_sample(page_table, top_idx, slopes, qs3, ks3, vs3, ck, cv, past_len)
            xp, xs = _matmul(op, o_s.reshape(nseq, d), attn_w_o, j, n_out=d, tn=1024,
                             res=(xp, xs))
            att_out += [kp.reshape(batch, seq, N_HEADS, hd), vp.reshape(batch, seq, N_HEADS, hd),
                        ks.reshape(nseq, 1, N_HEADS, hd), vs.reshape(nseq, 1, N_HEADS, hd)]
        else:
            ygp, ygs = _matmul(hp, hs, rg_w_gate, j, n_out=d, tn=1024, epilogue="gelu_tanh")
            xrp, xrs = _matmul(hp, hs, rg_w_x, j, n_out=d, tn=1024)
            conv_state = state_rglru_conv[j].transpose(1, 0, 2)
            gp, hl_p, gs, hl_s = _rglru(xrp, ygp, xrs, ygs, conv_state, state_rglru_h,
                                        rg_conv_w, rg_conv_b3, rg_w_a, rg_b_a3, rg_w_i, rg_b_i3,
                                        rg_lam3, j, batch, seq)
            xp, xs = _matmul(gp, gs, rg_w_out, j, n_out=d, tn=1024, res=(xp, xs))
            keep = rg_conv_w.shape[1] - 1
            rg_h_p.append(hl_p.reshape(batch, d))
            rg_h_s.append(hl_s)
            rg_c_p.append(xrp.reshape(batch, seq, d)[:, seq - keep:])
            rg_c_s.append(jnp.concatenate([state_rglru_conv[j][:, 1:], xrs[:, None, :]], axis=1))
        hp, hs = _rmsnorm(xp, xs, norm_ffn3, layer, BF16)
        f_state = state_ffn_conv[layer].transpose(1, 0, 2)
        up_p, up_s, tail, g_s = _ffn_up(hp, hs, ffn_w_gate, ffn_w_up, f_state, ffn_conv_w,
                                        ffn_conv_b3, layer, batch, seq, tn=512)
        xp, xs = _matmul(up_p, up_s, ffn_w_down, layer, n_out=d, tn=512, res=(xp, xs),
                         w_buffers=1)
        keep = ffn_conv_w.shape[1] - 1
        ffn_c_p.append(tail[:, SUBLANES - keep:])
        ffn_c_s.append(jnp.concatenate([state_ffn_conv[layer][:, 1:], g_s[:, None, :]], axis=1))

    yp, ys = _rmsnorm(xp, xs, norm_final.reshape(1, 1, d), 0, F32)
    return (yp.reshape(batch, seq, d), ys.reshape(nseq, 1, d), *att_out,
            jnp.stack(rg_h_p), jnp.stack(rg_h_s), jnp.stack(rg_c_p), jnp.stack(rg_c_s),
            jnp.stack(ffn_c_p), jnp.stack(ffn_c_s))
```

```python
import functools
import math

import jax
import jax.numpy as jnp
from jax import lax
from jax.experimental import pallas as pl
from jax.experimental.pallas import tpu as pltpu

F32 = jnp.float32
BF16 = jnp.bfloat16
HIGHEST = lax.Precision.HIGHEST

N_HEADS = 16
MOBA_BLOCK = 256
BLOCK_SHIFT = MOBA_BLOCK.bit_length() - 1
MOBA_TOP_K = 3
RG_C = 8.0
NORM_EPS = 1e-6
MASK_BIG = 2.0 ** 100
LOG2_E = math.log2(math.e)

V7X_VMEM_BYTES = 64 * 1024 * 1024
VMEM_LIMIT = V7X_VMEM_BYTES - 8 * 1024 * 1024
LANES = 128
SUBLANES = 8

NORM_ROWS = 1024
MM_ROWS = 1024


def _params(n_axes):
    return pltpu.CompilerParams(
        dimension_semantics=("arbitrary",) * n_axes, vmem_limit_bytes=VMEM_LIMIT)


def _norm_kernel(xp_ref, xs_ref, g_ref, op_ref, os_ref, *, nm):
    m = pl.program_id(0)
    g = g_ref[...]

    def nrm(x):
        y = x * lax.rsqrt(jnp.mean(x * x, axis=-1, keepdims=True) + NORM_EPS)
        return y * g

    @pl.when(m < nm)
    def _():
        op_ref[...] = nrm(xp_ref[...]).astype(op_ref.dtype)

    @pl.when(m == nm)
    def _():
        os_ref[...] = nrm(xs_ref[...]).astype(os_ref.dtype)


def _rmsnorm(xp, xs, gains, layer, out_dtype):
    mp, d = xp.shape
    ms = xs.shape[0]
    nm = mp // NORM_ROWS
    last = nm - 1
    return pl.pallas_call(
        functools.partial(_norm_kernel, nm=nm),
        grid=(nm + 1,),
        in_specs=[
            pl.BlockSpec((NORM_ROWS, d), lambda m: (jnp.minimum(m, last), 0)),
            pl.BlockSpec((ms, d), lambda m: (0, 0)),
            pl.BlockSpec((None, 1, d), lambda m: (layer, 0, 0)),
        ],
        out_specs=[
            pl.BlockSpec((NORM_ROWS, d), lambda m: (jnp.minimum(m, last), 0)),
            pl.BlockSpec((ms, d), lambda m: (0, 0)),
        ],
        out_shape=[jax.ShapeDtypeStruct((mp, d), out_dtype),
                   jax.ShapeDtypeStruct((ms, d), F32)],
        compiler_params=_params(1),
        name="rmsnorm",
    )(xp, xs, gains)


def _gelu_tanh(x):
    return jax.nn.gelu(x, approximate=True)


def _mm_kernel(*refs, nm, epilogue, has_res):
    if has_res:
        ap_ref, as_ref, w_ref, rp_ref, rs_ref, op_ref, os_ref, wb_ref = refs
    else:
        ap_ref, as_ref, w_ref, op_ref, os_ref, wb_ref = refs
        rp_ref = rs_ref = None
    m = pl.program_id(1)

    @pl.when(m == 0)
    def _():
        wb_ref[...] = w_ref[...].astype(BF16)

    def finish(acc, r_ref, o_ref):
        if epilogue == "gelu_tanh":
            acc = _gelu_tanh(acc)
        if r_ref is not None:
            acc = r_ref[...] + acc
        o_ref[...] = acc.astype(o_ref.dtype)

    @pl.when(m < nm)
    def _():
        acc = jnp.dot(ap_ref[...], wb_ref[...], preferred_element_type=F32)
        finish(acc, rp_ref, op_ref)

    @pl.when(m == nm)
    def _():
        acc = jnp.dot(as_ref[...].astype(BF16), wb_ref[...], preferred_element_type=F32)
        finish(acc, rs_ref, os_ref)


def _matmul(ap, a_s, w, layer, *, col_off=0, n_out, tn, epilogue=None, res=None,
            out_dtype=F32, w_buffers=2):
    mp, k = ap.shape
    ms = a_s.shape[0]
    tm = MM_ROWS
    nm = mp // tm
    last = nm - 1
    nn = n_out // tn
    off = col_off // tn
    in_specs = [
        pl.BlockSpec((tm, k), lambda n, m: (jnp.minimum(m, last), 0)),
        pl.BlockSpec((ms, k), lambda n, m: (0, 0)),
        pl.BlockSpec((None, k, tn), lambda n, m: (layer, 0, off + n),
                     pipeline_mode=pl.Buffered(w_buffers)),
    ]
    args = [ap, a_s, w]
    if res is not None:
        in_specs += [
            pl.BlockSpec((tm, tn), lambda n, m: (jnp.minimum(m, last), n)),
            pl.BlockSpec((ms, tn), lambda n, m: (0, n)),
        ]
        args += list(res)
    return pl.pallas_call(
        functools.partial(_mm_kernel, nm=nm, epilogue=epilogue, has_res=res is not None),
        grid=(nn, nm + 1),
        in_specs=in_specs,
        out_specs=[
            pl.BlockSpec((tm, tn), lambda n, m: (jnp.minimum(m, last), n)),
            pl.BlockSpec((ms, tn), lambda n, m: (0, n)),
        ],
        out_shape=[jax.ShapeDtypeStruct((mp, n_out), out_dtype),
                   jax.ShapeDtypeStruct((ms, n_out), F32)],
        scratch_shapes=[pltpu.VMEM((k, tn), BF16)],
        compiler_params=_params(2),
        name="matmul",
    )(*args)


def _moba_prompt_kernel(pt_ref, slopes_ref, q_ref, k_ref, v_ref, *refs, seq, hd, n_side,
                        pages_per_blk, steps_per_seq):
    page_refs = refs[:n_side]
    o_ref, km_ref, qa_scr, ka_scr, va_scr = refs[n_side:]
    nblk = seq // MOBA_BLOCK
    slope = slopes_ref[pl.program_id(1)]
    scale = 1.0 / math.sqrt(hd)

    step = pl.program_id(0) * pl.num_programs(1) + pl.program_id(1)
    blks = n_side // pages_per_blk
    rows = pages_per_blk * page_refs[0].shape[0]
    for b in range(blks):
        acc = jnp.sum(page_refs[b * pages_per_blk][...], axis=0)
        for p in range(1, pages_per_blk):
            acc = acc + jnp.sum(page_refs[b * pages_per_blk + p][...], axis=0)
        km_ref[(step % steps_per_seq) * blks + b] = acc * (1.0 / rows)

    q = q_ref[...]
    k = k_ref[...]

    kmean = jnp.mean(k.reshape(nblk, MOBA_BLOCK, hd), axis=1)
    gate = lax.dot_general(kmean.astype(BF16), q.astype(BF16), (((1,), (1,)), ((), ())),
                           preferred_element_type=F32)
    n_idx = lax.broadcasted_iota(jnp.int32, (nblk, seq), 0)
    own = lax.broadcasted_iota(jnp.int32, (nblk, seq), 1) >> BLOCK_SHIFT
    g = jnp.where(n_idx < own, gate, -jnp.inf)
    sel = n_idx == own
    for _ in range(MOBA_TOP_K):
        mx = jnp.max(g, axis=0, keepdims=True)
        first = jnp.min(jnp.where(g == mx, n_idx, nblk), axis=0, keepdims=True)
        pick = (n_idx == first) & (mx > -jnp.inf)
        sel = sel | pick
        g = jnp.where(pick, -jnp.inf, g)

    assert nblk <= SUBLANES
    pos = (lax.broadcasted_iota(jnp.int32, (1, seq), 1) - seq // 2).astype(F32)
    bias = (slope * LOG2_E) * pos
    b0 = bias.astype(BF16).astype(F32)
    rem = bias - b0
    b1 = rem.astype(BF16).astype(F32)
    b2 = (rem - b1).astype(BF16).astype(F32)
    r8 = lax.broadcasted_iota(jnp.int32, (SUBLANES, seq), 0)
    pieces = jnp.where(r8 == 0, b0, jnp.where(r8 == 1, b1, b2))
    q_ext = jnp.where(r8 < 3, -pieces, jnp.where(r8 < 6, 1.0, 0.0))
    k_ext = jnp.where(r8 < 3, 1.0, jnp.where(r8 < 6, jnp.where(r8 == 3, b0, jnp.where(r8 == 4, b1, b2)),
                                             0.0))
    key_blk = lax.broadcasted_iota(jnp.int32, (SUBLANES, seq), 1) >> BLOCK_SHIFT
    zeros = jnp.zeros((LANES - 2 * SUBLANES, seq), F32)
    q_rows = jnp.concatenate([jnp.where(sel, 0.0, 1.0), q_ext, zeros], axis=0)
    k_rows = jnp.concatenate([jnp.where(r8 == key_blk, -MASK_BIG, 0.0), k_ext, zeros], axis=0)
    qa_scr[:, 0:hd] = (q * (scale * LOG2_E)).astype(BF16)
    qa_scr[:, hd:] = q_rows.T.astype(BF16)
    ka_scr[:, 0:hd] = k.astype(BF16)
    ka_scr[:, hd:] = k_rows.T.astype(BF16)
    va_scr[:, 0:hd] = v_ref[...].astype(BF16)
    va_scr[:, hd:] = jnp.where(
        lax.broadcasted_iota(jnp.int32, (seq, LANES), 1) == 0, 1.0, 0.0).astype(BF16)
    causal = jnp.where(
        lax.broadcasted_iota(jnp.int32, (MOBA_BLOCK, MOBA_BLOCK), 1)
        <= lax.broadcasted_iota(jnp.int32, (MOBA_BLOCK, MOBA_BLOCK), 0), 0.0, -MASK_BIG)

    def scores(i):
        lo, hi = i * MOBA_BLOCK, (i + 1) * MOBA_BLOCK
        return lax.dot_general(qa_scr[lo:hi, :], ka_scr[0:hi, :], (((1,), (1,)), ((), ())),
                               preferred_element_type=F32)

    def probs(i, s):
        lo, hi = i * MOBA_BLOCK, (i + 1) * MOBA_BLOCK
        s_own = s[:, lo:hi] + causal
        mx = jnp.max(s_own, axis=1, keepdims=True)
        if i == 0:
            return jnp.exp2(s_own - mx).astype(BF16), None
        mx = jnp.maximum(mx, jnp.max(s[:, 0:lo], axis=1, keepdims=True))
        return jnp.exp2(s_own - mx).astype(BF16), jnp.exp2(s[:, 0:lo] - mx).astype(BF16)

    def emit(i, p_own, p_past):
        lo, hi = i * MOBA_BLOCK, (i + 1) * MOBA_BLOCK
        pv = jnp.dot(p_own, va_scr[lo:hi, :], preferred_element_type=F32)
        if p_past is not None:
            pv = pv + jnp.dot(p_past, va_scr[0:lo, :], preferred_element_type=F32)
        o_ref[lo:hi, :] = (pv[:, 0:hd] / pv[:, hd:hd + 1]).astype(o_ref.dtype)

    s_cur = p_cur = None
    for t in range(nblk + 2):
        s_new = scores(t) if t < nblk else None
        p_new = probs(t - 1, s_cur) if 1 <= t <= nblk else None
        if t >= 2:
            emit(t - 2, *p_cur)
        s_cur, p_cur = s_new, p_new


def _moba_prompt(slopes, q, k, v, batch, seq, cache_k, page_table):
    d = q.shape[1]
    hd = d // N_HEADS
    _, page, h, _ = cache_k.shape
    nseq, n_pages = page_table.shape
    per_blk = MOBA_BLOCK // page
    steps = batch * N_HEADS
    n_side = (nseq * n_pages) // steps
    assert MOBA_BLOCK % page == 0 and n_side * steps == nseq * n_pages
    assert n_pages % n_side == 0 and n_side % per_blk == 0
    sps = n_pages // n_side

    def page_spec(t):
        def index(b, hh, pt):
            step = b * N_HEADS + hh
            return (pt[step // sps, (step % sps) * n_side + t], 0, 0, 0)
        return pl.BlockSpec((None, page, h, hd), index)

    spec = pl.BlockSpec((seq, hd), lambda b, hh, pt: (b, hh))
    grid_spec = pltpu.PrefetchScalarGridSpec(
        num_scalar_prefetch=1,
        grid=(batch, N_HEADS),
        in_specs=([pl.BlockSpec(memory_space=pltpu.SMEM), spec, spec, spec]
                  + [page_spec(t) for t in range(n_side)]),
        out_specs=[spec,
                   pl.BlockSpec((None, n_pages // per_blk, h, hd),
                                lambda b, hh, pt: ((b * N_HEADS + hh) // sps, 0, 0, 0))],
        scratch_shapes=[pltpu.VMEM((seq, hd + LANES), BF16)] * 3,
    )
    return pl.pallas_call(
        functools.partial(_moba_prompt_kernel, seq=seq, hd=hd, n_side=n_side,
                          pages_per_blk=per_blk, steps_per_seq=sps),
        grid_spec=grid_spec,
        out_shape=[jax.ShapeDtypeStruct((batch * seq, d), BF16),
                   jax.ShapeDtypeStruct((nseq, n_pages // per_blk, h, hd), F32)],
        compiler_params=_params(2),
        name="moba_prompt",
    )(page_table, slopes, q, k, v, *([cache_k] * n_side))


def _select_kernel(q_ref, km_ref, o_ref):
    nblk, h, _ = km_ref.shape
    km = km_ref[...].astype(BF16).astype(F32)
    qb = q_ref[...].astype(BF16).astype(F32)
    g = jnp.sum(km * qb[None], axis=-1)
    n_idx = lax.broadcasted_iota(jnp.int32, (nblk, h), 0)
    rows = []
    for _ in range(MOBA_TOP_K):
        mx = jnp.max(g, axis=0, keepdims=True)
        first = jnp.min(jnp.where(g == mx, n_idx, nblk), axis=0, keepdims=True)
        rows.append(first)
        g = jnp.where(n_idx == first, -jnp.inf, g)
    rows.append(jnp.zeros((SUBLANES - MOBA_TOP_K, h), jnp.int32))
    o_ref[...] = jnp.concatenate(rows, axis=0)


def _select_blocks(q_s, kmean):
    nseq, nblk, h, hd = kmean.shape
    return pl.pallas_call(
        _select_kernel,
        grid=(nseq,),
        in_specs=[pl.BlockSpec((None, h, hd), lambda s: (s, 0, 0)),
                  pl.BlockSpec((None, nblk, h, hd), lambda s: (s, 0, 0, 0))],
        out_specs=pl.BlockSpec((None, SUBLANES, h), lambda s: (s, 0, 0)),
        out_shape=jax.ShapeDtypeStruct((nseq, SUBLANES, h), jnp.int32),
        compiler_params=_params(1),
        name="select_blocks",
    )(q_s, kmean)


def _moba_sample_kernel(pt_ref, idx_ref, slopes_ref, q_ref, kn_ref, vn_ref, kc_hbm, vc_hbm,
                        o_ref, kd_scr, vd_scr, sem, *, page, past_len, hd):
    per_blk = MOBA_BLOCK // page
    n_pg = MOBA_TOP_K * per_blk
    n_heads = q_ref.shape[0]
    s_id = pl.program_id(0)
    scale = 1.0 / math.sqrt(hd)

    def block_of(h, r):
        return idx_ref[(s_id * n_heads + h) * MOBA_TOP_K + r]

    def head_copies(h):
        out = []
        for t in range(n_pg):
            r, half = divmod(t, per_blk)
            pg = pt_ref[s_id, per_blk * block_of(h, r) + half]
            out.append(pltpu.make_async_copy(kc_hbm.at[pg, :, h, :], kd_scr.at[h, t], sem.at[h]))
            out.append(pltpu.make_async_copy(vc_hbm.at[pg, :, h, :], vd_scr.at[h, t], sem.at[h]))
        return out

    def start_head(h, carry):
        for cp in head_copies(h):
            cp.start()
        return carry

    lax.fori_loop(0, n_heads, start_head, 0)
    row = lax.broadcasted_iota(jnp.int32, (page, 1), 0)

    def attend_head(h, carry):
        for cp in head_copies(h):
            cp.wait()
        slope = slopes_ref[h]
        q = q_ref[pl.ds(h, 1), :]
        logits = []
        for t in range(n_pg):
            r, half = divmod(t, per_blk)
            s = jnp.sum(kd_scr[h, t] * q, axis=1, keepdims=True)
            dist = (past_len - block_of(h, r) * MOBA_BLOCK - half * page - row).astype(F32)
            logits.append(s * scale - slope * dist)
        l_self = jnp.sum(q * kn_ref[pl.ds(h, 1), :], axis=1, keepdims=True) * scale
        mx = l_self
        for l in logits:
            mx = jnp.maximum(mx, jnp.max(l, axis=0, keepdims=True))
        p_self = jnp.exp(l_self - mx)
        denom = p_self
        acc = p_self * vn_ref[pl.ds(h, 1), :]
        for t in range(n_pg):
            p = jnp.exp(logits[t] - mx)
            denom = denom + jnp.sum(p, axis=0, keepdims=True)
            acc = acc + jnp.sum(p * vd_scr[h, t], axis=0, keepdims=True)
        o_ref[pl.ds(h, 1), :] = acc / denom
        return carry

    lax.fori_loop(0, n_heads, attend_head, 0)


def _moba_sample(page_table, top_idx, slopes, q_s, k_s, v_s, cache_k, cache_v, past_len):
    _, page, h, hd = cache_k.shape
    nseq = q_s.shape[0]
    n_pg = MOBA_TOP_K * (MOBA_BLOCK // page)
    seq_spec = pl.BlockSpec((None, h, hd), lambda s, pt, idx: (s, 0, 0))
    grid_spec = pltpu.PrefetchScalarGridSpec(
        num_scalar_prefetch=2,
        grid=(nseq,),
        in_specs=[pl.BlockSpec(memory_space=pltpu.SMEM), seq_spec, seq_spec, seq_spec,
                  pl.BlockSpec(memory_space=pl.ANY), pl.BlockSpec(memory_space=pl.ANY)],
        out_specs=seq_spec,
        scratch_shapes=[pltpu.VMEM((h, n_pg, page, hd), F32), pltpu.VMEM((h, n_pg, page, hd), F32),
                        pltpu.SemaphoreType.DMA((h,))],
    )
    return pl.pallas_call(
        functools.partial(_moba_sample_kernel, page=page, past_len=past_len, hd=hd),
        grid_spec=grid_spec,
        out_shape=jax.ShapeDtypeStruct((nseq, h, hd), F32),
        compiler_params=_params(1),
        name="moba_sample",
    )(page_table, top_idx, slopes, q_s, k_s, v_s, cache_k, cache_v)


def _sigmoid(x):
    return 0.5 * jnp.tanh(0.5 * x) + 0.5


def _rg_gates(xc, wa_ref, ba_ref, wi_ref, bi_ref, lam_ref):
    xcb = xc.astype(BF16)
    r = _sigmoid(
        jnp.dot(xcb, wa_ref[...].astype(BF16), preferred_element_type=F32) + ba_ref[...])
    i = _sigmoid(
        jnp.dot(xcb, wi_ref[...].astype(BF16), preferred_element_type=F32) + bi_ref[...])
    neg_lam = -lam_ref[...]
    softplus = jnp.maximum(neg_lam, 0.0) + jnp.log1p(jnp.exp(-jnp.abs(neg_lam)))
    log_a = -RG_C * r * softplus
    a = jnp.exp(log_a)
    mult = jnp.sqrt(jnp.tanh(-log_a) * (a * a + 1.0))
    return a, mult * (i * xc)


def _rglru_kernel(xr_ref, yg_ref, xrs_ref, ygs_ref, cs_ref, h0_ref, cw_ref, cb_ref,
                  wa_ref, ba_ref, wi_ref, bi_ref, lam_ref,
                  o_ref, hl_ref, os_ref, hs_ref, xs_scr, a_scr, b_scr, h_scr, *, nb, seq):
    b = pl.program_id(1)
    cw = cw_ref[...]
    width = cw.shape[0]
    gate_refs = (wa_ref, ba_ref, wi_ref, bi_ref, lam_ref)

    @pl.when(b < nb)
    def _():
        x = xr_ref[...]
        xs_scr[0:SUBLANES, :] = jnp.zeros((SUBLANES, x.shape[1]), F32)
        xs_scr[SUBLANES:, :] = x
        xc = xs_scr[pl.ds(SUBLANES - (width - 1), seq), :] * cw[0:1]
        for j in range(1, width):
            xc = xc + xs_scr[pl.ds(SUBLANES - (width - 1) + j, seq), :] * cw[j:j + 1]
        xc = xc + cb_ref[...]
        a, bx = _rg_gates(xc, *gate_refs)
        a_scr[...] = a
        b_scr[...] = bx
        row = lax.broadcasted_iota(jnp.int32, (SUBLANES, x.shape[1]), 0)

        def body(g, h):
            off = pl.multiple_of(g * SUBLANES, SUBLANES)
            aa = a_scr[pl.ds(off, SUBLANES), :]
            bb = b_scr[pl.ds(off, SUBLANES), :]
            for d in (1, 2, 4):
                a_sh = jnp.where(row >= d, pltpu.roll(aa, d, 0), 1.0)
                b_sh = jnp.where(row >= d, pltpu.roll(bb, d, 0), 0.0)
                bb = aa * b_sh + bb
                aa = aa * a_sh
            hh = aa * h + bb
            h_scr[pl.ds(off, SUBLANES), :] = hh
            return hh[SUBLANES - 1:SUBLANES, :]

        h_last = lax.fori_loop(0, seq // SUBLANES, body, jnp.zeros((1, x.shape[1]), F32),
                               unroll=4)
        hl_ref[...] = h_last
        o_ref[...] = (h_scr[...] * yg_ref[...]).astype(o_ref.dtype)

    @pl.when(b == nb)
    def _():
        xc = cs_ref[0] * cw[0:1]
        for j in range(1, width - 1):
            xc = xc + cs_ref[j] * cw[j:j + 1]
        xc = xc + xrs_ref[...] * cw[width - 1:width] + cb_ref[...]
        a, bx = _rg_gates(xc, *gate_refs)
        hh = a * h0_ref[...] + bx
        hs_ref[...] = hh
        os_ref[...] = hh * ygs_ref[...]


def _rglru(xr, yg, xr_s, yg_s, conv_state, h0, conv_w, conv_b, w_a, b_a, w_i, b_i, lam, j,
           batch, seq):
    d = xr.shape[1]
    ms = xr_s.shape[0]
    nblk, cw = w_a.shape[1], w_a.shape[2]
    width = conv_w.shape[1]
    last = batch - 1
    seq_spec = pl.BlockSpec((seq, cw), lambda c, b: (jnp.minimum(b, last), c))
    smp_spec = pl.BlockSpec((ms, cw), lambda c, b: (0, c))
    vec_spec = pl.BlockSpec((None, 1, cw), lambda c, b: (j, 0, c))
    mat_spec = pl.BlockSpec((None, None, cw, cw), lambda c, b: (j, c, 0, 0))
    return pl.pallas_call(
        functools.partial(_rglru_kernel, nb=batch, seq=seq),
        grid=(nblk, batch + 1),
        in_specs=[
            seq_spec, seq_spec, smp_spec, smp_spec,
            pl.BlockSpec((width - 1, ms, cw), lambda c, b: (0, 0, c)),
            pl.BlockSpec((None, ms, cw), lambda c, b: (j, 0, c)),
            pl.BlockSpec((None, width, cw), lambda c, b: (j, 0, c)),
            vec_spec, mat_spec, vec_spec, mat_spec, vec_spec, vec_spec,
        ],
        out_specs=[
            seq_spec,
            pl.BlockSpec((None, 1, cw), lambda c, b: (jnp.minimum(b, last), 0, c)),
            smp_spec, smp_spec,
        ],
        out_shape=[
            jax.ShapeDtypeStruct((batch * seq, d), BF16),
            jax.ShapeDtypeStruct((batch, 1, d), F32),
            jax.ShapeDtypeStruct((ms, d), F32),
            jax.ShapeDtypeStruct((ms, d), F32),
        ],
        scratch_shapes=[pltpu.VMEM((seq + SUBLANES, cw), F32),
                        pltpu.VMEM((seq, cw), F32),
                        pltpu.VMEM((seq, cw), F32),
                        pltpu.VMEM((seq, cw), F32)],
        compiler_params=_params(2),
        name="rglru",
    )(xr, yg, xr_s, yg_s, conv_state, h0, conv_w, conv_b, w_a, b_a, w_i, b_i, lam)


def _gelu_erf(x):
    return 0.5 * x * (1.0 + lax.erf(x * math.sqrt(0.5)))


def _ffn_up_kernel(ap_ref, as_ref, wg_ref, wu_ref, s_ref, cw_ref, cb_ref,
                   hp_ref, hs_ref, tail_ref, gs_ref, wgb_ref, wub_ref, g_scr, *, nm, tiles_per_seq):
    m = pl.program_id(1)
    cw = cw_ref[...]
    width = cw.shape[0]
    tm = ap_ref.shape[0]

    @pl.when(m == 0)
    def _():
        wgb_ref[...] = wg_ref[...].astype(BF16)
        wub_ref[...] = wu_ref[...].astype(BF16)
        g_scr[tm:, :] = jnp.zeros((SUBLANES, g_scr.shape[1]), F32)

    @pl.when(m < nm)
    def _():
        a = ap_ref[...]
        g = jnp.dot(a, wgb_ref[...], preferred_element_type=F32)
        g_scr[0:SUBLANES, :] = jnp.where(m % tiles_per_seq == 0, 0.0, g_scr[tm:, :])
        g_scr[SUBLANES:, :] = g
        y = g_scr[pl.ds(SUBLANES - (width - 1), tm), :] * cw[0:1]
        for j in range(1, width - 1):
            y = y + g_scr[pl.ds(SUBLANES - (width - 1) + j, tm), :] * cw[j:j + 1]
        y = y + g * cw[width - 1:width] + cb_ref[...]
        act = _gelu_erf(y)
        u = jnp.dot(a, wub_ref[...], preferred_element_type=F32)
        hp_ref[...] = (act * u).astype(hp_ref.dtype)
        tail_ref[...] = g[tm - SUBLANES:, :]

    @pl.when(m == nm)
    def _():
        a = as_ref[...].astype(BF16)
        g = jnp.dot(a, wgb_ref[...], preferred_element_type=F32)
        u = jnp.dot(a, wub_ref[...], preferred_element_type=F32)
        y = s_ref[0] * cw[0:1]
        for j in range(1, width - 1):
            y = y + s_ref[j] * cw[j:j + 1]
        y = y + g * cw[width - 1:width] + cb_ref[...]
        hs_ref[...] = _gelu_erf(y) * u
        gs_ref[...] = g


def _ffn_up(ap, a_s, w_gate, w_up, state, conv_w, conv_b, layer, batch, seq, tn):
    mp, k = ap.shape
    ms = a_s.shape[0]
    f = w_gate.shape[2]
    width = conv_w.shape[1]
    tm = MM_ROWS
    nm = mp // tm
    last = nm - 1
    tps = seq // tm
    w_spec = pl.BlockSpec((None, k, tn), lambda n, m: (layer, 0, n))
    return pl.pallas_call(
        functools.partial(_ffn_up_kernel, nm=nm, tiles_per_seq=tps),
        grid=(f // tn, nm + 1),
        in_specs=[
            pl.BlockSpec((tm, k), lambda n, m: (jnp.minimum(m, last), 0)),
            pl.BlockSpec((ms, k), lambda n, m: (0, 0)),
            w_spec, w_spec,
            pl.BlockSpec((width - 1, ms, tn), lambda n, m: (0, 0, n)),
            pl.BlockSpec((None, width, tn), lambda n, m: (layer, 0, n)),
            pl.BlockSpec((None, 1, tn), lambda n, m: (layer, 0, n)),
        ],
        out_specs=[
            pl.BlockSpec((tm, tn), lambda n, m: (jnp.minimum(m, last), n)),
            pl.BlockSpec((ms, tn), lambda n, m: (0, n)),
            pl.BlockSpec((None, SUBLANES, tn), lambda n, m: (jnp.minimum(m, last) // tps, 0, n)),
            pl.BlockSpec((ms, tn), lambda n, m: (0, n)),
        ],
        out_shape=[
            jax.ShapeDtypeStruct((mp, f), BF16),
            jax.ShapeDtypeStruct((ms, f), F32),
            jax.ShapeDtypeStruct((batch, SUBLANES, f), F32),
            jax.ShapeDtypeStruct((ms, f), F32),
        ],
        scratch_shapes=[pltpu.VMEM((k, tn), BF16), pltpu.VMEM((k, tn), BF16),
                        pltpu.VMEM((tm + SUBLANES, tn), F32)],
        compiler_params=_params(2),
        name="ffn_up",
    )(ap, a_s, w_gate, w_up, state, conv_w, conv_b)


def kernel(x_prompt, x_sample, cache_k_l0, cache_v_l0, cache_k_l2, cache_v_l2, page_table, state_rglru_h, state_rglru_conv, state_ffn_conv, norm_mix, norm_ffn, norm_final, attn_w_qkv, attn_w_o, rg_w_gate, rg_w_x, rg_conv_w, rg_conv_b, rg_w_a, rg_b_a, rg_w_i, rg_b_i, rg_lambda, rg_w_out, ffn_w_gate, ffn_w_up, ffn_conv_w, ffn_conv_b, ffn_w_down):
    batch, seq, d = x_prompt.shape
    nseq, dec_seq, _ = x_sample.shape
    assert dec_seq == 1 and seq % MM_ROWS == 0 and seq % MOBA_BLOCK == 0
    depth = norm_mix.shape[0]
    hd = d // N_HEADS
    page = cache_k_l0.shape[1]
    past_len = page_table.shape[1] * page
    assert past_len % MOBA_BLOCK == 0
    d_ff = ffn_w_gate.shape[2]
    mp = batch * seq

    xp = x_prompt.reshape(mp, d)
    xs = x_sample.reshape(nseq, d)
    slopes = jnp.exp2(-8.0 * jnp.arange(1, N_HEADS + 1, dtype=F32) / N_HEADS)
    caches = [(cache_k_l0, cache_v_l0), (cache_k_l2, cache_v_l2)]
    norm_mix3 = norm_mix.reshape(depth, 1, d)
    norm_ffn3 = norm_ffn.reshape(depth, 1, d)
    rg_conv_b3 = rg_conv_b.reshape(-1, 1, d)
    rg_b_a3 = rg_b_a.reshape(-1, 1, d)
    rg_b_i3 = rg_b_i.reshape(-1, 1, d)
    rg_lam3 = rg_lambda.reshape(-1, 1, d)
    ffn_conv_b3 = ffn_conv_b.reshape(depth, 1, d_ff)

    att_out, rg_h_p, rg_h_s, rg_c_p, rg_c_s, ffn_c_p, ffn_c_s = [], [], [], [], [], [], []
    for layer in range(depth):
        j = layer // 2
        hp, hs = _rmsnorm(xp, xs, norm_mix3, layer, BF16)
        if layer % 2 == 0:
            ck, cv = caches[j]
            qp, qs = _matmul(hp, hs, attn_w_qkv, j, col_off=0, n_out=d, tn=1024)
            kp, ks = _matmul(hp, hs, attn_w_qkv, j, col_off=d, n_out=d, tn=1024)
            vp, vs = _matmul(hp, hs, attn_w_qkv, j, col_off=2 * d, n_out=d, tn=1024)
            op, kmean = _moba_prompt(slopes, qp, kp, vp, batch, seq, ck, page_table)
            qs3, ks3, vs3 = (t.reshape(nseq, N_HEADS, hd) for t in (qs, ks, vs))
            picks = _select_blocks(qs3, kmean)
            top_idx = picks[:, :MOBA_TOP_K].transpose(0, 2, 1).reshape(-1)
            o_s = _moba_sample(page_table, top_idx, slopes, qs3, ks3, vs3, ck, cv, past_len)
            xp, xs = _matmul(op, o_s.reshape(nseq, d), attn_w_o, j, n_out=d, tn=1024,
                             res=(xp, xs))
            att_out += [kp.reshape(batch, seq, N_HEADS, hd), vp.reshape(batch, seq, N_HEADS, hd),
                        ks.reshape(nseq, 1, N_HEADS, hd), vs.reshape(nseq, 1, N_HEADS, hd)]
        else:
            ygp, ygs = _matmul(hp, hs, rg_w_gate, j, n_out=d, tn=1024, epilogue="gelu_tanh",
                               out_dtype=BF16)
            xrp, xrs = _matmul(hp, hs, rg_w_x, j, n_out=d, tn=1024)
            conv_state = state_rglru_conv[j].transpose(1, 0, 2)
            gp, hl_p, gs, hl_s = _rglru(xrp, ygp, xrs, ygs, conv_state, state_rglru_h,
                                        rg_conv_w, rg_conv_b3, rg_w_a, rg_b_a3, rg_w_i, rg_b_i3,
                                        rg_lam3, j, batch, seq)
            xp, xs = _matmul(gp, gs, rg_w_out, j, n_out=d, tn=1024, res=(xp, xs))
            keep = rg_conv_w.shape[1] - 1
            rg_h_p.append(hl_p.reshape(batch, d))
            rg_h_s.append(hl_s)
            rg_c_p.append(xrp.reshape(batch, seq, d)[:, seq - keep:])
            rg_c_s.append(jnp.concatenate([state_rglru_conv[j][:, 1:], xrs[:, None, :]], axis=1))
        hp, hs = _rmsnorm(xp, xs, norm_ffn3, layer, BF16)
        f_state = state_ffn_conv[layer].transpose(1, 0, 2)
        up_p, up_s, tail, g_s = _ffn_up(hp, hs, ffn_w_gate, ffn_w_up, f_state, ffn_conv_w,
                                        ffn_conv_b3, layer, batch, seq, tn=512)
        xp, xs = _matmul(up_p, up_s, ffn_w_down, layer, n_out=d, tn=512, res=(xp, xs),
                         w_buffers=1)
        keep = ffn_conv_w.shape[1] - 1
        ffn_c_p.append(tail[:, SUBLANES - keep:])
        ffn_c_s.append(jnp.concatenate([state_ffn_conv[layer][:, 1:], g_s[:, None, :]], axis=1))

    yp, ys = _rmsnorm(xp, xs, norm_final.reshape(1, 1, d), 0, F32)
    return (yp.reshape(batch, seq, d), ys.reshape(nseq, 1, d), *att_out,
            jnp.stack(rg_h_p), jnp.stack(rg_h_s), jnp.stack(rg_c_p), jnp.stack(rg_c_s),
            jnp.stack(ffn_c_p), jnp.stack(ffn_c_s))
```

```python
import functools
import math

import jax
import jax.numpy as jnp
from jax import lax
from jax.experimental import pallas as pl
from jax.experimental.pallas import tpu as pltpu

F32 = jnp.float32
BF16 = jnp.bfloat16
HIGHEST = lax.Precision.HIGHEST

N_HEADS = 16
MOBA_BLOCK = 256
BLOCK_SHIFT = MOBA_BLOCK.bit_length() - 1
MOBA_TOP_K = 3
RG_C = 8.0
NORM_EPS = 1e-6
MASK_BIG = 2.0 ** 100
LOG2_E = math.log2(math.e)

V7X_VMEM_BYTES = 64 * 1024 * 1024
VMEM_LIMIT = V7X_VMEM_BYTES - 8 * 1024 * 1024
LANES = 128
SUBLANES = 8

NORM_ROWS = 1024
MM_ROWS = 1024
FFN_ROWS = 2048


def _params(n_axes):
    return pltpu.CompilerParams(
        dimension_semantics=("arbitrary",) * n_axes, vmem_limit_bytes=VMEM_LIMIT)


def _norm_kernel(xp_ref, xs_ref, g_ref, op_ref, os_ref, *, nm):
    m = pl.program_id(0)
    g = g_ref[...]

    def nrm(x):
        y = x * lax.rsqrt(jnp.mean(x * x, axis=-1, keepdims=True) + NORM_EPS)
        return y * g

    @pl.when(m < nm)
    def _():
        op_ref[...] = nrm(xp_ref[...]).astype(op_ref.dtype)

    @pl.when(m == nm)
    def _():
        os_ref[...] = nrm(xs_ref[...]).astype(os_ref.dtype)


def _rmsnorm(xp, xs, gains, layer, out_dtype):
    mp, d = xp.shape
    ms = xs.shape[0]
    nm = mp // NORM_ROWS
    last = nm - 1
    return pl.pallas_call(
        functools.partial(_norm_kernel, nm=nm),
        grid=(nm + 1,),
        in_specs=[
            pl.BlockSpec((NORM_ROWS, d), lambda m: (jnp.minimum(m, last), 0)),
            pl.BlockSpec((ms, d), lambda m: (0, 0)),
            pl.BlockSpec((None, 1, d), lambda m: (layer, 0, 0)),
        ],
        out_specs=[
            pl.BlockSpec((NORM_ROWS, d), lambda m: (jnp.minimum(m, last), 0)),
            pl.BlockSpec((ms, d), lambda m: (0, 0)),
        ],
        out_shape=[jax.ShapeDtypeStruct((mp, d), out_dtype),
                   jax.ShapeDtypeStruct((ms, d), F32)],
        compiler_params=_params(1),
        name="rmsnorm",
    )(xp, xs, gains)


def _gelu_tanh(x):
    return jax.nn.gelu(x, approximate=True)


def _mm_kernel(*refs, nm, epilogue, has_res):
    if has_res:
        ap_ref, as_ref, w_ref, rp_ref, rs_ref, op_ref, os_ref, wb_ref = refs
    else:
        ap_ref, as_ref, w_ref, op_ref, os_ref, wb_ref = refs
        rp_ref = rs_ref = None
    m = pl.program_id(1)

    @pl.when(m == 0)
    def _():
        wb_ref[...] = w_ref[...].astype(BF16)

    def finish(acc, r_ref, o_ref):
        if epilogue == "gelu_tanh":
            acc = _gelu_tanh(acc)
        if r_ref is not None:
            acc = r_ref[...] + acc
        o_ref[...] = acc.astype(o_ref.dtype)

    @pl.when(m < nm)
    def _():
        acc = jnp.dot(ap_ref[...], wb_ref[...], preferred_element_type=F32)
        finish(acc, rp_ref, op_ref)

    @pl.when(m == nm)
    def _():
        acc = jnp.dot(as_ref[...].astype(BF16), wb_ref[...], preferred_element_type=F32)
        finish(acc, rs_ref, os_ref)


def _matmul(ap, a_s, w, layer, *, col_off=0, n_out, tn, epilogue=None, res=None,
            out_dtype=F32, w_buffers=2):
    mp, k = ap.shape
    ms = a_s.shape[0]
    tm = MM_ROWS
    nm = mp // tm
    last = nm - 1
    nn = n_out // tn
    off = col_off // tn
    in_specs = [
        pl.BlockSpec((tm, k), lambda n, m: (jnp.minimum(m, last), 0)),
        pl.BlockSpec((ms, k), lambda n, m: (0, 0)),
        pl.BlockSpec((None, k, tn), lambda n, m: (layer, 0, off + n),
                     pipeline_mode=pl.Buffered(w_buffers)),
    ]
    args = [ap, a_s, w]
    if res is not None:
        in_specs += [
            pl.BlockSpec((tm, tn), lambda n, m: (jnp.minimum(m, last), n)),
            pl.BlockSpec((ms, tn), lambda n, m: (0, n)),
        ]
        args += list(res)
    return pl.pallas_call(
        functools.partial(_mm_kernel, nm=nm, epilogue=epilogue, has_res=res is not None),
        grid=(nn, nm + 1),
        in_specs=in_specs,
        out_specs=[
            pl.BlockSpec((tm, tn), lambda n, m: (jnp.minimum(m, last), n)),
            pl.BlockSpec((ms, tn), lambda n, m: (0, n)),
        ],
        out_shape=[jax.ShapeDtypeStruct((mp, n_out), out_dtype),
                   jax.ShapeDtypeStruct((ms, n_out), F32)],
        scratch_shapes=[pltpu.VMEM((k, tn), BF16)],
        compiler_params=_params(2),
        name="matmul",
    )(*args)


def _moba_prompt_kernel(pt_ref, slopes_ref, q_ref, k_ref, v_ref, *refs, seq, hd, n_side,
                        pages_per_blk, steps_per_seq):
    page_refs = refs[:n_side]
    o_ref, km_ref, qa_scr, ka_scr, va_scr = refs[n_side:]
    nblk = seq // MOBA_BLOCK
    slope = slopes_ref[pl.program_id(1)]
    scale = 1.0 / math.sqrt(hd)

    step = pl.program_id(0) * pl.num_programs(1) + pl.program_id(1)
    blks = n_side // pages_per_blk
    rows = pages_per_blk * page_refs[0].shape[0]
    for b in range(blks):
        acc = jnp.sum(page_refs[b * pages_per_blk][...], axis=0)
        for p in range(1, pages_per_blk):
            acc = acc + jnp.sum(page_refs[b * pages_per_blk + p][...], axis=0)
        km_ref[(step % steps_per_seq) * blks + b] = acc * (1.0 / rows)

    q = q_ref[...]
    k = k_ref[...]

    kmean = jnp.mean(k.reshape(nblk, MOBA_BLOCK, hd), axis=1)
    gate = lax.dot_general(kmean.astype(BF16), q.astype(BF16), (((1,), (1,)), ((), ())),
                           preferred_element_type=F32)
    n_idx = lax.broadcasted_iota(jnp.int32, (nblk, seq), 0)
    own = lax.broadcasted_iota(jnp.int32, (nblk, seq), 1) >> BLOCK_SHIFT
    g = jnp.where(n_idx < own, gate, -jnp.inf)
    sel = n_idx == own
    for _ in range(MOBA_TOP_K):
        mx = jnp.max(g, axis=0, keepdims=True)
        first = jnp.min(jnp.where(g == mx, n_idx, nblk), axis=0, keepdims=True)
        pick = (n_idx == first) & (mx > -jnp.inf)
        sel = sel | pick
        g = jnp.where(pick, -jnp.inf, g)

    assert nblk <= SUBLANES
    pos = (lax.broadcasted_iota(jnp.int32, (1, seq), 1) - seq // 2).astype(F32)
    bias = (slope * LOG2_E) * pos
    b0 = bias.astype(BF16).astype(F32)
    rem = bias - b0
    b1 = rem.astype(BF16).astype(F32)
    b2 = (rem - b1).astype(BF16).astype(F32)
    r8 = lax.broadcasted_iota(jnp.int32, (SUBLANES, seq), 0)
    pieces = jnp.where(r8 == 0, b0, jnp.where(r8 == 1, b1, b2))
    q_ext = jnp.where(r8 < 3, -pieces, jnp.where(r8 < 6, 1.0, 0.0))
    k_ext = jnp.where(r8 < 3, 1.0, jnp.where(r8 < 6, jnp.where(r8 == 3, b0, jnp.where(r8 == 4, b1, b2)),
                                             0.0))
    key_blk = lax.broadcasted_iota(jnp.int32, (SUBLANES, seq), 1) >> BLOCK_SHIFT
    zeros = jnp.zeros((LANES - 2 * SUBLANES, seq), F32)
    q_rows = jnp.concatenate([jnp.where(sel, 0.0, 1.0), q_ext, zeros], axis=0)
    k_rows = jnp.concatenate([jnp.where(r8 == key_blk, -MASK_BIG, 0.0), k_ext, zeros], axis=0)
    qa_scr[:, 0:hd] = (q * (scale * LOG2_E)).astype(BF16)
    qa_scr[:, hd:] = q_rows.T.astype(BF16)
    ka_scr[:, 0:hd] = k.astype(BF16)
    ka_scr[:, hd:] = k_rows.T.astype(BF16)
    va_scr[:, 0:hd] = v_ref[...].astype(BF16)
    va_scr[:, hd:] = jnp.where(
        lax.broadcasted_iota(jnp.int32, (seq, LANES), 1) == 0, 1.0, 0.0).astype(BF16)
    causal = jnp.where(
        lax.broadcasted_iota(jnp.int32, (MOBA_BLOCK, MOBA_BLOCK), 1)
        <= lax.broadcasted_iota(jnp.int32, (MOBA_BLOCK, MOBA_BLOCK), 0), 0.0, -MASK_BIG)

    def scores(i):
        lo, hi = i * MOBA_BLOCK, (i + 1) * MOBA_BLOCK
        return lax.dot_general(qa_scr[lo:hi, :], ka_scr[0:hi, :], (((1,), (1,)), ((), ())),
                               preferred_element_type=F32)

    def probs(i, s):
        lo, hi = i * MOBA_BLOCK, (i + 1) * MOBA_BLOCK
        s_own = s[:, lo:hi] + causal
        mx = jnp.max(s_own, axis=1, keepdims=True)
        if i == 0:
            return jnp.exp2(s_own - mx).astype(BF16), None
        mx = jnp.maximum(mx, jnp.max(s[:, 0:lo], axis=1, keepdims=True))
        return jnp.exp2(s_own - mx).astype(BF16), jnp.exp2(s[:, 0:lo] - mx).astype(BF16)

    def emit(i, p_own, p_past):
        lo, hi = i * MOBA_BLOCK, (i + 1) * MOBA_BLOCK
        pv = jnp.dot(p_own, va_scr[lo:hi, :], preferred_element_type=F32)
        if p_past is not None:
            pv = pv + jnp.dot(p_past, va_scr[0:lo, :], preferred_element_type=F32)
        o_ref[lo:hi, :] = (pv[:, 0:hd] / pv[:, hd:hd + 1]).astype(o_ref.dtype)

    s_cur = p_cur = None
    for t in range(nblk + 2):
        s_new = scores(t) if t < nblk else None
        p_new = probs(t - 1, s_cur) if 1 <= t <= nblk else None
        if t >= 2:
            emit(t - 2, *p_cur)
        s_cur, p_cur = s_new, p_new


def _moba_prompt(slopes, q, k, v, batch, seq, cache_k, page_table):
    d = q.shape[1]
    hd = d // N_HEADS
    _, page, h, _ = cache_k.shape
    nseq, n_pages = page_table.shape
    per_blk = MOBA_BLOCK // page
    steps = batch * N_HEADS
    n_side = (nseq * n_pages) // steps
    assert MOBA_BLOCK % page == 0 and n_side * steps == nseq * n_pages
    assert n_pages % n_side == 0 and n_side % per_blk == 0
    sps = n_pages // n_side

    def page_spec(t):
        def index(b, hh, pt):
            step = b * N_HEADS + hh
            return (pt[step // sps, (step % sps) * n_side + t], 0, 0, 0)
        return pl.BlockSpec((None, page, h, hd), index)

    spec = pl.BlockSpec((seq, hd), lambda b, hh, pt: (b, hh))
    grid_spec = pltpu.PrefetchScalarGridSpec(
        num_scalar_prefetch=1,
        grid=(batch, N_HEADS),
        in_specs=([pl.BlockSpec(memory_space=pltpu.SMEM), spec, spec, spec]
                  + [page_spec(t) for t in range(n_side)]),
        out_specs=[spec,
                   pl.BlockSpec((None, n_pages // per_blk, h, hd),
                                lambda b, hh, pt: ((b * N_HEADS + hh) // sps, 0, 0, 0))],
        scratch_shapes=[pltpu.VMEM((seq, hd + LANES), BF16)] * 3,
    )
    return pl.pallas_call(
        functools.partial(_moba_prompt_kernel, seq=seq, hd=hd, n_side=n_side,
                          pages_per_blk=per_blk, steps_per_seq=sps),
        grid_spec=grid_spec,
        out_shape=[jax.ShapeDtypeStruct((batch * seq, d), BF16),
                   jax.ShapeDtypeStruct((nseq, n_pages // per_blk, h, hd), F32)],
        compiler_params=_params(2),
        name="moba_prompt",
    )(page_table, slopes, q, k, v, *([cache_k] * n_side))


def _bf16_round(x):
    return x.astype(BF16).astype(F32)


def _select_kernel(q_ref, km_ref, o_ref):
    nblk, h, _ = km_ref.shape
    g = jnp.sum(_bf16_round(km_ref[...]) * _bf16_round(q_ref[...])[None], axis=-1)
    n_idx = lax.broadcasted_iota(jnp.int32, (nblk, h), 0)
    rows = []
    for _ in range(MOBA_TOP_K):
        mx = jnp.max(g, axis=0, keepdims=True)
        first = jnp.min(jnp.where(g == mx, n_idx, nblk), axis=0, keepdims=True)
        rows.append(first)
        g = jnp.where(n_idx == first, -jnp.inf, g)
    rows.append(jnp.zeros((SUBLANES - MOBA_TOP_K, h), jnp.int32))
    o_ref[...] = jnp.concatenate(rows, axis=0)


def _select_blocks(q_s, kmean):
    nseq, nblk, h, hd = kmean.shape
    return pl.pallas_call(
        _select_kernel,
        grid=(nseq,),
        in_specs=[pl.BlockSpec((None, h, hd), lambda s: (s, 0, 0)),
                  pl.BlockSpec((None, nblk, h, hd), lambda s: (s, 0, 0, 0))],
        out_specs=pl.BlockSpec((None, SUBLANES, h), lambda s: (s, 0, 0)),
        out_shape=jax.ShapeDtypeStruct((nseq, SUBLANES, h), jnp.int32),
        compiler_params=_params(1),
        name="select_blocks",
    )(q_s, kmean)


def _moba_sample_kernel(pt_ref, idx_ref, slopes_ref, q_ref, kn_ref, vn_ref, kc_hbm, vc_hbm,
                        o_ref, kd_scr, vd_scr, sem, *, page, past_len, hd):
    per_blk = MOBA_BLOCK // page
    n_pg = MOBA_TOP_K * per_blk
    n_heads = q_ref.shape[0]
    s_id = pl.program_id(0)
    scale = 1.0 / math.sqrt(hd)

    def block_of(h, r):
        return idx_ref[(s_id * n_heads + h) * MOBA_TOP_K + r]

    def head_copies(h):
        out = []
        for t in range(n_pg):
            r, half = divmod(t, per_blk)
            pg = pt_ref[s_id, per_blk * block_of(h, r) + half]
            out.append(pltpu.make_async_copy(kc_hbm.at[pg, :, h, :], kd_scr.at[h, t], sem.at[h]))
            out.append(pltpu.make_async_copy(vc_hbm.at[pg, :, h, :], vd_scr.at[h, t], sem.at[h]))
        return out

    def start_head(h, carry):
        for cp in head_copies(h):
            cp.start()
        return carry

    lax.fori_loop(0, n_heads, start_head, 0)
    row = lax.broadcasted_iota(jnp.int32, (page, 1), 0)

    def attend_head(h, carry):
        for cp in head_copies(h):
            cp.wait()
        slope = slopes_ref[h]
        q = _bf16_round(q_ref[pl.ds(h, 1), :])
        logits = []
        for t in range(n_pg):
            r, half = divmod(t, per_blk)
            s = jnp.sum(_bf16_round(kd_scr[h, t]) * q, axis=1, keepdims=True)
            dist = (past_len - block_of(h, r) * MOBA_BLOCK - half * page - row).astype(F32)
            logits.append(s * scale - slope * dist)
        l_self = jnp.sum(q * _bf16_round(kn_ref[pl.ds(h, 1), :]), axis=1, keepdims=True) * scale
        mx = l_self
        for l in logits:
            mx = jnp.maximum(mx, jnp.max(l, axis=0, keepdims=True))
        p_self = jnp.exp(l_self - mx)
        probs = [jnp.exp(l - mx) for l in logits]
        denom = p_self
        for p in probs:
            denom = denom + jnp.sum(p, axis=0, keepdims=True)
        acc = _bf16_round(p_self / denom) * _bf16_round(vn_ref[pl.ds(h, 1), :])
        for t in range(n_pg):
            acc = acc + jnp.sum(_bf16_round(probs[t] / denom) * _bf16_round(vd_scr[h, t]),
                                axis=0, keepdims=True)
        o_ref[pl.ds(h, 1), :] = acc
        return carry

    lax.fori_loop(0, n_heads, attend_head, 0)


def _moba_sample(page_table, top_idx, slopes, q_s, k_s, v_s, cache_k, cache_v, past_len):
    _, page, h, hd = cache_k.shape
    nseq = q_s.shape[0]
    n_pg = MOBA_TOP_K * (MOBA_BLOCK // page)
    seq_spec = pl.BlockSpec((None, h, hd), lambda s, pt, idx: (s, 0, 0))
    grid_spec = pltpu.PrefetchScalarGridSpec(
        num_scalar_prefetch=2,
        grid=(nseq,),
        in_specs=[pl.BlockSpec(memory_space=pltpu.SMEM), seq_spec, seq_spec, seq_spec,
                  pl.BlockSpec(memory_space=pl.ANY), pl.BlockSpec(memory_space=pl.ANY)],
        out_specs=seq_spec,
        scratch_shapes=[pltpu.VMEM((h, n_pg, page, hd), F32), pltpu.VMEM((h, n_pg, page, hd), F32),
                        pltpu.SemaphoreType.DMA((h,))],
    )
    return pl.pallas_call(
        functools.partial(_moba_sample_kernel, page=page, past_len=past_len, hd=hd),
        grid_spec=grid_spec,
        out_shape=jax.ShapeDtypeStruct((nseq, h, hd), F32),
        compiler_params=_params(1),
        name="moba_sample",
    )(page_table, top_idx, slopes, q_s, k_s, v_s, cache_k, cache_v)


def _sigmoid(x):
    return 0.5 * jnp.tanh(0.5 * x) + 0.5


def _rg_gates(xc, wa_ref, ba_ref, wi_ref, bi_ref, lam_ref):
    xcb = xc.astype(BF16)
    r = _sigmoid(
        jnp.dot(xcb, wa_ref[...].astype(BF16), preferred_element_type=F32) + ba_ref[...])
    i = _sigmoid(
        jnp.dot(xcb, wi_ref[...].astype(BF16), preferred_element_type=F32) + bi_ref[...])
    neg_lam = -lam_ref[...]
    softplus = jnp.maximum(neg_lam, 0.0) + jnp.log1p(jnp.exp(-jnp.abs(neg_lam)))
    log_a = -RG_C * r * softplus
    a = jnp.exp(log_a)
    mult = jnp.sqrt(jnp.tanh(-log_a) * (a * a + 1.0))
    return a, mult * (i * xc)


def _rglru_kernel(xr_ref, yg_ref, xrs_ref, ygs_ref, cs_ref, h0_ref, cw_ref, cb_ref,
                  wa_ref, ba_ref, wi_ref, bi_ref, lam_ref,
                  o_ref, hl_ref, os_ref, hs_ref, xs_scr, a_scr, b_scr, h_scr, *, nb, seq):
    b = pl.program_id(1)
    cw = cw_ref[...]
    width = cw.shape[0]
    gate_refs = (wa_ref, ba_ref, wi_ref, bi_ref, lam_ref)

    @pl.when(b < nb)
    def _():
        x = xr_ref[...]
        xs_scr[0:SUBLANES, :] = jnp.zeros((SUBLANES, x.shape[1]), F32)
        xs_scr[SUBLANES:, :] = x
        xc = xs_scr[pl.ds(SUBLANES - (width - 1), seq), :] * cw[0:1]
        for j in range(1, width):
            xc = xc + xs_scr[pl.ds(SUBLANES - (width - 1) + j, seq), :] * cw[j:j + 1]
        xc = xc + cb_ref[...]
        a, bx = _rg_gates(xc, *gate_refs)
        a_scr[...] = a
        b_scr[...] = bx
        row = lax.broadcasted_iota(jnp.int32, (SUBLANES, x.shape[1]), 0)

        def body(g, h):
            off = pl.multiple_of(g * SUBLANES, SUBLANES)
            aa = a_scr[pl.ds(off, SUBLANES), :]
            bb = b_scr[pl.ds(off, SUBLANES), :]
            for d in (1, 2, 4):
                a_sh = jnp.where(row >= d, pltpu.roll(aa, d, 0), 1.0)
                b_sh = jnp.where(row >= d, pltpu.roll(bb, d, 0), 0.0)
                bb = aa * b_sh + bb
                aa = aa * a_sh
            hh = aa * h + bb
            h_scr[pl.ds(off, SUBLANES), :] = hh
            return hh[SUBLANES - 1:SUBLANES, :]

        h_last = lax.fori_loop(0, seq // SUBLANES, body, jnp.zeros((1, x.shape[1]), F32),
                               unroll=4)
        hl_ref[...] = h_last
        o_ref[...] = (h_scr[...] * yg_ref[...]).astype(o_ref.dtype)

    @pl.when(b == nb)
    def _():
        xc = cs_ref[0] * cw[0:1]
        for j in range(1, width - 1):
            xc = xc + cs_ref[j] * cw[j:j + 1]
        xc = xc + xrs_ref[...] * cw[width - 1:width] + cb_ref[...]
        a, bx = _rg_gates(xc, *gate_refs)
        hh = a * h0_ref[...] + bx
        hs_ref[...] = hh
        os_ref[...] = hh * ygs_ref[...]


def _rglru(xr, yg, xr_s, yg_s, conv_state, h0, conv_w, conv_b, w_a, b_a, w_i, b_i, lam, j,
           batch, seq):
    d = xr.shape[1]
    ms = xr_s.shape[0]
    nblk, cw = w_a.shape[1], w_a.shape[2]
    width = conv_w.shape[1]
    last = batch - 1
    seq_spec = pl.BlockSpec((seq, cw), lambda c, b: (jnp.minimum(b, last), c))
    smp_spec = pl.BlockSpec((ms, cw), lambda c, b: (0, c))
    vec_spec = pl.BlockSpec((None, 1, cw), lambda c, b: (j, 0, c))
    mat_spec = pl.BlockSpec((None, None, cw, cw), lambda c, b: (j, c, 0, 0))
    return pl.pallas_call(
        functools.partial(_rglru_kernel, nb=batch, seq=seq),
        grid=(nblk, batch + 1),
        in_specs=[
            seq_spec, seq_spec, smp_spec, smp_spec,
            pl.BlockSpec((width - 1, ms, cw), lambda c, b: (0, 0, c)),
            pl.BlockSpec((None, ms, cw), lambda c, b: (j, 0, c)),
            pl.BlockSpec((None, width, cw), lambda c, b: (j, 0, c)),
            vec_spec, mat_spec, vec_spec, mat_spec, vec_spec, vec_spec,
        ],
        out_specs=[
            seq_spec,
            pl.BlockSpec((None, 1, cw), lambda c, b: (jnp.minimum(b, last), 0, c)),
            smp_spec, smp_spec,
        ],
        out_shape=[
            jax.ShapeDtypeStruct((batch * seq, d), BF16),
            jax.ShapeDtypeStruct((batch, 1, d), F32),
            jax.ShapeDtypeStruct((ms, d), F32),
            jax.ShapeDtypeStruct((ms, d), F32),
        ],
        scratch_shapes=[pltpu.VMEM((seq + SUBLANES, cw), F32),
                        pltpu.VMEM((seq, cw), F32),
                        pltpu.VMEM((seq, cw), F32),
                        pltpu.VMEM((seq, cw), F32)],
        compiler_params=_params(2),
        name="rglru",
    )(xr, yg, xr_s, yg_s, conv_state, h0, conv_w, conv_b, w_a, b_a, w_i, b_i, lam)


def _gelu_erf(x):
    return 0.5 * x * (1.0 + lax.erf(x * math.sqrt(0.5)))


def _ffn_up_kernel(ap_ref, as_ref, wg_ref, wu_ref, s_ref, cw_ref, cb_ref,
                   hp_ref, hs_ref, tail_ref, gs_ref, wgb_ref, wub_ref, g_scr, *, nm, tiles_per_seq):
    m = pl.program_id(1)
    cw = cw_ref[...]
    width = cw.shape[0]
    tm = ap_ref.shape[0]

    @pl.when(m == 0)
    def _():
        wgb_ref[...] = wg_ref[...].astype(BF16)
        wub_ref[...] = wu_ref[...].astype(BF16)
        g_scr[tm:, :] = jnp.zeros((SUBLANES, g_scr.shape[1]), F32)

    @pl.when(m < nm)
    def _():
        g_scr[0:SUBLANES, :] = jnp.where(m % tiles_per_seq == 0, 0.0, g_scr[tm:, :])
        for r0 in range(0, tm, MM_ROWS):
            a = ap_ref[r0:r0 + MM_ROWS, :]
            g = jnp.dot(a, wgb_ref[...], preferred_element_type=F32)
            g_scr[SUBLANES + r0:SUBLANES + r0 + MM_ROWS, :] = g
            first = SUBLANES + r0 - (width - 1)
            y = g_scr[pl.ds(first, MM_ROWS), :] * cw[0:1]
            for j in range(1, width - 1):
                y = y + g_scr[pl.ds(first + j, MM_ROWS), :] * cw[j:j + 1]
            y = y + g * cw[width - 1:width] + cb_ref[...]
            act = _gelu_erf(y)
            u = jnp.dot(a, wub_ref[...], preferred_element_type=F32)
            hp_ref[r0:r0 + MM_ROWS, :] = (act * u).astype(hp_ref.dtype)
        tail_ref[...] = g_scr[tm:, :]

    @pl.when(m == nm)
    def _():
        a = as_ref[...].astype(BF16)
        g = jnp.dot(a, wgb_ref[...], preferred_element_type=F32)
        u = jnp.dot(a, wub_ref[...], preferred_element_type=F32)
        y = s_ref[0] * cw[0:1]
        for j in range(1, width - 1):
            y = y + s_ref[j] * cw[j:j + 1]
        y = y + g * cw[width - 1:width] + cb_ref[...]
        hs_ref[...] = _gelu_erf(y) * u
        gs_ref[...] = g


def _ffn_up(ap, a_s, w_gate, w_up, state, conv_w, conv_b, layer, batch, seq, tn):
    mp, k = ap.shape
    ms = a_s.shape[0]
    f = w_gate.shape[2]
    width = conv_w.shape[1]
    tm = FFN_ROWS
    assert seq % tm == 0 and tm % MM_ROWS == 0
    nm = mp // tm
    last = nm - 1
    tps = seq // tm
    w_spec = pl.BlockSpec((None, k, tn), lambda n, m: (layer, 0, n))
    return pl.pallas_call(
        functools.partial(_ffn_up_kernel, nm=nm, tiles_per_seq=tps),
        grid=(f // tn, nm + 1),
        in_specs=[
            pl.BlockSpec((tm, k), lambda n, m: (jnp.minimum(m, last), 0)),
            pl.BlockSpec((ms, k), lambda n, m: (0, 0)),
            w_spec, w_spec,
            pl.BlockSpec((width - 1, ms, tn), lambda n, m: (0, 0, n)),
            pl.BlockSpec((None, width, tn), lambda n, m: (layer, 0, n)),
            pl.BlockSpec((None, 1, tn), lambda n, m: (layer, 0, n)),
        ],
        out_specs=[
            pl.BlockSpec((tm, tn), lambda n, m: (jnp.minimum(m, last), n)),
            pl.BlockSpec((ms, tn), lambda n, m: (0, n)),
            pl.BlockSpec((None, SUBLANES, tn), lambda n, m: (jnp.minimum(m, last) // tps, 0, n)),
            pl.BlockSpec((ms, tn), lambda n, m: (0, n)),
        ],
        out_shape=[
            jax.ShapeDtypeStruct((mp, f), BF16),
            jax.ShapeDtypeStruct((ms, f), F32),
            jax.ShapeDtypeStruct((batch, SUBLANES, f), F32),
            jax.ShapeDtypeStruct((ms, f), F32),
        ],
        scratch_shapes=[pltpu.VMEM((k, tn), BF16), pltpu.VMEM((k, tn), BF16),
                        pltpu.VMEM((tm + SUBLANES, tn), F32)],
        compiler_params=_params(2),
        name="ffn_up",
    )(ap, a_s, w_gate, w_up, state, conv_w, conv_b)


def kernel(x_prompt, x_sample, cache_k_l0, cache_v_l0, cache_k_l2, cache_v_l2, page_table, state_rglru_h, state_rglru_conv, state_ffn_conv, norm_mix, norm_ffn, norm_final, attn_w_qkv, attn_w_o, rg_w_gate, rg_w_x, rg_conv_w, rg_conv_b, rg_w_a, rg_b_a, rg_w_i, rg_b_i, rg_lambda, rg_w_out, ffn_w_gate, ffn_w_up, ffn_conv_w, ffn_conv_b, ffn_w_down):
    batch, seq, d = x_prompt.shape
    nseq, dec_seq, _ = x_sample.shape
    assert dec_seq == 1 and seq % MM_ROWS == 0 and seq % MOBA_BLOCK == 0
    depth = norm_mix.shape[0]
    hd = d // N_HEADS
    page = cache_k_l0.shape[1]
    past_len = page_table.shape[1] * page
    assert past_len % MOBA_BLOCK == 0
    d_ff = ffn_w_gate.shape[2]
    mp = batch * seq

    xp = x_prompt.reshape(mp, d)
    xs = x_sample.reshape(nseq, d)
    slopes = jnp.exp2(-8.0 * jnp.arange(1, N_HEADS + 1, dtype=F32) / N_HEADS)
    caches = [(cache_k_l0, cache_v_l0), (cache_k_l2, cache_v_l2)]
    norm_mix3 = norm_mix.reshape(depth, 1, d)
    norm_ffn3 = norm_ffn.reshape(depth, 1, d)
    rg_conv_b3 = rg_conv_b.reshape(-1, 1, d)
    rg_b_a3 = rg_b_a.reshape(-1, 1, d)
    rg_b_i3 = rg_b_i.reshape(-1, 1, d)
    rg_lam3 = rg_lambda.reshape(-1, 1, d)
    ffn_conv_b3 = ffn_conv_b.reshape(depth, 1, d_ff)

    att_out, rg_h_p, rg_h_s, rg_c_p, rg_c_s, ffn_c_p, ffn_c_s = [], [], [], [], [], [], []
    for layer in range(depth):
        j = layer // 2
        hp, hs = _rmsnorm(xp, xs, norm_mix3, layer, BF16)
        if layer % 2 == 0:
            ck, cv = caches[j]
            qp, qs = _matmul(hp, hs, attn_w_qkv, j, col_off=0, n_out=d, tn=1024)
            kp, ks = _matmul(hp, hs, attn_w_qkv, j, col_off=d, n_out=d, tn=1024)
            vp, vs = _matmul(hp, hs, attn_w_qkv, j, col_off=2 * d, n_out=d, tn=1024)
            op, kmean = _moba_prompt(slopes, qp, kp, vp, batch, seq, ck, page_table)
            qs3, ks3, vs3 = (t.reshape(nseq, N_HEADS, hd) for t in (qs, ks, vs))
            picks = _select_blocks(qs3, kmean)
            top_idx = picks[:, :MOBA_TOP_K].transpose(0, 2, 1).reshape(-1)
            o_s = _moba_sample(page_table, top_idx, slopes, qs3, ks3, vs3, ck, cv, past_len)
            xp, xs = _matmul(op, o_s.reshape(nseq, d), attn_w_o, j, n_out=d, tn=1024,
                             res=(xp, xs))
            att_out += [kp.reshape(batch, seq, N_HEADS, hd), vp.reshape(batch, seq, N_HEADS, hd),
                        ks.reshape(nseq, 1, N_HEADS, hd), vs.reshape(nseq, 1, N_HEADS, hd)]
        else:
            ygp, ygs = _matmul(hp, hs, rg_w_gate, j, n_out=d, tn=1024, epilogue="gelu_tanh",
                               out_dtype=BF16)
            xrp, xrs = _matmul(hp, hs, rg_w_x, j, n_out=d, tn=1024)
            conv_state = state_rglru_conv[j].transpose(1, 0, 2)
            gp, hl_p, gs, hl_s = _rglru(xrp, ygp, xrs, ygs, conv_state, state_rglru_h,
                                        rg_conv_w, rg_conv_b3, rg_w_a, rg_b_a3, rg_w_i, rg_b_i3,
                                        rg_lam3, j, batch, seq)
            xp, xs = _matmul(gp, gs, rg_w_out, j, n_out=d, tn=1024, res=(xp, xs))
            keep = rg_conv_w.shape[1] - 1
            rg_h_p.append(hl_p.reshape(batch, d))
            rg_h_s.append(hl_s)
            rg_c_p.append(xrp.reshape(batch, seq, d)[:, seq - keep:])
            rg_c_s.append(jnp.concatenate([state_rglru_conv[j][:, 1:], xrs[:, None, :]], axis=1))
        hp, hs = _rmsnorm(xp, xs, norm_ffn3, layer, BF16)
        f_state = state_ffn_conv[layer].transpose(1, 0, 2)
        up_p, up_s, tail, g_s = _ffn_up(hp, hs, ffn_w_gate, ffn_w_up, f_state, ffn_conv_w,
                                        ffn_conv_b3, layer, batch, seq, tn=512)
        xp, xs = _matmul(up_p, up_s, ffn_w_down, layer, n_out=d, tn=512, res=(xp, xs),
                         w_buffers=1)
        keep = ffn_conv_w.shape[1] - 1
        ffn_c_p.append(tail[:, SUBLANES - keep:])
        ffn_c_s.append(jnp.concatenate([state_ffn_conv[layer][:, 1:], g_s[:, None, :]], axis=1))

    yp, ys = _rmsnorm(xp, xs, norm_final.reshape(1, 1, d), 0, F32)
    return (yp.reshape(batch, seq, d), ys.reshape(nseq, 1, d), *att_out,
            jnp.stack(rg_h_p), jnp.stack(rg_h_s), jnp.stack(rg_c_p), jnp.stack(rg_c_s),
            jnp.stack(ffn_c_p), jnp.stack(ffn_c_s))
```

```python
import functools
import math

import jax
import jax.numpy as jnp
from jax import lax
from jax.experimental import pallas as pl
from jax.experimental.pallas import tpu as pltpu

F32 = jnp.float32
BF16 = jnp.bfloat16
HIGHEST = lax.Precision.HIGHEST

N_HEADS = 16
MOBA_BLOCK = 256
BLOCK_SHIFT = MOBA_BLOCK.bit_length() - 1
MOBA_TOP_K = 3
RG_C = 8.0
NORM_EPS = 1e-6
MASK_BIG = 2.0 ** 100
LOG2_E = math.log2(math.e)

V7X_VMEM_BYTES = 64 * 1024 * 1024
VMEM_LIMIT = V7X_VMEM_BYTES - 8 * 1024 * 1024
LANES = 128
SUBLANES = 8

NORM_ROWS = 1024
MM_ROWS = 1024
FFN_ROWS = 1024


def _params(n_axes):
    return pltpu.CompilerParams(
        dimension_semantics=("arbitrary",) * n_axes, vmem_limit_bytes=VMEM_LIMIT)


def _norm_kernel(xp_ref, xs_ref, g_ref, op_ref, os_ref, *, nm):
    m = pl.program_id(0)
    g = g_ref[...]

    def nrm(x):
        y = x * lax.rsqrt(jnp.mean(x * x, axis=-1, keepdims=True) + NORM_EPS)
        return y * g

    @pl.when(m < nm)
    def _():
        op_ref[...] = nrm(xp_ref[...]).astype(op_ref.dtype)

    @pl.when(m == nm)
    def _():
        os_ref[...] = nrm(xs_ref[...]).astype(os_ref.dtype)


def _rmsnorm(xp, xs, gains, layer, out_dtype):
    mp, d = xp.shape
    ms = xs.shape[0]
    nm = mp // NORM_ROWS
    last = nm - 1
    return pl.pallas_call(
        functools.partial(_norm_kernel, nm=nm),
        grid=(nm + 1,),
        in_specs=[
            pl.BlockSpec((NORM_ROWS, d), lambda m: (jnp.minimum(m, last), 0)),
            pl.BlockSpec((ms, d), lambda m: (0, 0)),
            pl.BlockSpec((None, 1, d), lambda m: (layer, 0, 0)),
        ],
        out_specs=[
            pl.BlockSpec((NORM_ROWS, d), lambda m: (jnp.minimum(m, last), 0)),
            pl.BlockSpec((ms, d), lambda m: (0, 0)),
        ],
        out_shape=[jax.ShapeDtypeStruct((mp, d), out_dtype),
                   jax.ShapeDtypeStruct((ms, d), F32)],
        compiler_params=_params(1),
        name="rmsnorm",
    )(xp, xs, gains)


def _gelu_tanh(x):
    return jax.nn.gelu(x, approximate=True)


def _mm_kernel(*refs, nm, epilogue, has_res):
    if has_res:
        ap_ref, as_ref, w_ref, rp_ref, rs_ref, op_ref, os_ref, wb_ref = refs
    else:
        ap_ref, as_ref, w_ref, op_ref, os_ref, wb_ref = refs
        rp_ref = rs_ref = None
    m = pl.program_id(1)

    @pl.when(m == 0)
    def _():
        wb_ref[...] = w_ref[...].astype(BF16)

    def finish(acc, r_ref, o_ref):
        if epilogue == "gelu_tanh":
            acc = _gelu_tanh(acc)
        if r_ref is not None:
            acc = r_ref[...] + acc
        o_ref[...] = acc.astype(o_ref.dtype)

    @pl.when(m < nm)
    def _():
        acc = jnp.dot(ap_ref[...], wb_ref[...], preferred_element_type=F32)
        finish(acc, rp_ref, op_ref)

    @pl.when(m == nm)
    def _():
        acc = jnp.dot(as_ref[...].astype(BF16), wb_ref[...], preferred_element_type=F32)
        finish(acc, rs_ref, os_ref)


def _matmul(ap, a_s, w, layer, *, col_off=0, n_out, tn, epilogue=None, res=None,
            out_dtype=F32, w_buffers=2):
    mp, k = ap.shape
    ms = a_s.shape[0]
    tm = MM_ROWS
    nm = mp // tm
    last = nm - 1
    nn = n_out // tn
    off = col_off // tn
    in_specs = [
        pl.BlockSpec((tm, k), lambda n, m: (jnp.minimum(m, last), 0)),
        pl.BlockSpec((ms, k), lambda n, m: (0, 0)),
        pl.BlockSpec((None, k, tn), lambda n, m: (layer, 0, off + n),
                     pipeline_mode=pl.Buffered(w_buffers)),
    ]
    args = [ap, a_s, w]
    if res is not None:
        in_specs += [
            pl.BlockSpec((tm, tn), lambda n, m: (jnp.minimum(m, last), n)),
            pl.BlockSpec((ms, tn), lambda n, m: (0, n)),
        ]
        args += list(res)
    return pl.pallas_call(
        functools.partial(_mm_kernel, nm=nm, epilogue=epilogue, has_res=res is not None),
        grid=(nn, nm + 1),
        in_specs=in_specs,
        out_specs=[
            pl.BlockSpec((tm, tn), lambda n, m: (jnp.minimum(m, last), n)),
            pl.BlockSpec((ms, tn), lambda n, m: (0, n)),
        ],
        out_shape=[jax.ShapeDtypeStruct((mp, n_out), out_dtype),
                   jax.ShapeDtypeStruct((ms, n_out), F32)],
        scratch_shapes=[pltpu.VMEM((k, tn), BF16)],
        compiler_params=_params(2),
        name="matmul",
    )(*args)


def _moba_prompt_kernel(pt_ref, slopes_ref, q_ref, k_ref, v_ref, *refs, seq, hd, n_side,
                        pages_per_blk, steps_per_seq):
    page_refs = refs[:n_side]
    o_ref, km_ref, qa_scr, ka_scr, va_scr = refs[n_side:]
    nblk = seq // MOBA_BLOCK
    slope = slopes_ref[pl.program_id(1)]
    scale = 1.0 / math.sqrt(hd)

    step = pl.program_id(0) * pl.num_programs(1) + pl.program_id(1)
    blks = n_side // pages_per_blk
    rows = pages_per_blk * page_refs[0].shape[0]
    for b in range(blks):
        acc = jnp.sum(page_refs[b * pages_per_blk][...], axis=0)
        for p in range(1, pages_per_blk):
            acc = acc + jnp.sum(page_refs[b * pages_per_blk + p][...], axis=0)
        km_ref[(step % steps_per_seq) * blks + b] = acc * (1.0 / rows)

    q = q_ref[...]
    k = k_ref[...]

    kmean = jnp.mean(k.reshape(nblk, MOBA_BLOCK, hd), axis=1)
    gate = lax.dot_general(kmean.astype(BF16), q.astype(BF16), (((1,), (1,)), ((), ())),
                           preferred_element_type=F32)
    n_idx = lax.broadcasted_iota(jnp.int32, (nblk, seq), 0)
    own = lax.broadcasted_iota(jnp.int32, (nblk, seq), 1) >> BLOCK_SHIFT
    g = jnp.where(n_idx < own, gate, -jnp.inf)
    sel = n_idx == own
    for _ in range(MOBA_TOP_K):
        mx = jnp.max(g, axis=0, keepdims=True)
        first = jnp.min(jnp.where(g == mx, n_idx, nblk), axis=0, keepdims=True)
        pick = (n_idx == first) & (mx > -jnp.inf)
        sel = sel | pick
        g = jnp.where(pick, -jnp.inf, g)

    assert nblk <= SUBLANES
    pos = (lax.broadcasted_iota(jnp.int32, (1, seq), 1) - seq // 2).astype(F32)
    bias = (slope * LOG2_E) * pos
    b0 = bias.astype(BF16).astype(F32)
    rem = bias - b0
    b1 = rem.astype(BF16).astype(F32)
    b2 = (rem - b1).astype(BF16).astype(F32)
    r8 = lax.broadcasted_iota(jnp.int32, (SUBLANES, seq), 0)
    pieces = jnp.where(r8 == 0, b0, jnp.where(r8 == 1, b1, b2))
    q_ext = jnp.where(r8 < 3, -pieces, jnp.where(r8 < 6, 1.0, 0.0))
    k_ext = jnp.where(r8 < 3, 1.0, jnp.where(r8 < 6, jnp.where(r8 == 3, b0, jnp.where(r8 == 4, b1, b2)),
                                             0.0))
    key_blk = lax.broadcasted_iota(jnp.int32, (SUBLANES, seq), 1) >> BLOCK_SHIFT
    zeros = jnp.zeros((LANES - 2 * SUBLANES, seq), F32)
    q_rows = jnp.concatenate([jnp.where(sel, 0.0, 1.0), q_ext, zeros], axis=0)
    k_rows = jnp.concatenate([jnp.where(r8 == key_blk, -MASK_BIG, 0.0), k_ext, zeros], axis=0)
    qa_scr[:, 0:hd] = (q * (scale * LOG2_E)).astype(BF16)
    qa_scr[:, hd:] = q_rows.T.astype(BF16)
    ka_scr[:, 0:hd] = k.astype(BF16)
    ka_scr[:, hd:] = k_rows.T.astype(BF16)
    va_scr[:, 0:hd] = v_ref[...].astype(BF16)
    va_scr[:, hd:] = jnp.where(
        lax.broadcasted_iota(jnp.int32, (seq, LANES), 1) == 0, 1.0, 0.0).astype(BF16)
    causal = jnp.where(
        lax.broadcasted_iota(jnp.int32, (MOBA_BLOCK, MOBA_BLOCK), 1)
        <= lax.broadcasted_iota(jnp.int32, (MOBA_BLOCK, MOBA_BLOCK), 0), 0.0, -MASK_BIG)

    def scores(i):
        lo, hi = i * MOBA_BLOCK, (i + 1) * MOBA_BLOCK
        return lax.dot_general(qa_scr[lo:hi, :], ka_scr[0:hi, :], (((1,), (1,)), ((), ())),
                               preferred_element_type=F32)

    def probs(i, s):
        lo, hi = i * MOBA_BLOCK, (i + 1) * MOBA_BLOCK
        s_own = s[:, lo:hi] + causal
        mx = jnp.max(s_own, axis=1, keepdims=True)
        if i == 0:
            return jnp.exp2(s_own - mx).astype(BF16), None
        mx = jnp.maximum(mx, jnp.max(s[:, 0:lo], axis=1, keepdims=True))
        return jnp.exp2(s_own - mx).astype(BF16), jnp.exp2(s[:, 0:lo] - mx).astype(BF16)

    def emit(i, p_own, p_past):
        lo, hi = i * MOBA_BLOCK, (i + 1) * MOBA_BLOCK
        pv = jnp.dot(p_own, va_scr[lo:hi, :], preferred_element_type=F32)
        if p_past is not None:
            pv = pv + jnp.dot(p_past, va_scr[0:lo, :], preferred_element_type=F32)
        o_ref[lo:hi, :] = (pv[:, 0:hd] / pv[:, hd:hd + 1]).astype(o_ref.dtype)

    s_cur = p_cur = None
    for t in range(nblk + 2):
        s_new = scores(t) if t < nblk else None
        p_new = probs(t - 1, s_cur) if 1 <= t <= nblk else None
        if t >= 2:
            emit(t - 2, *p_cur)
        s_cur, p_cur = s_new, p_new


def _moba_prompt(slopes, q, k, v, batch, seq, cache_k, page_table):
    d = q.shape[1]
    hd = d // N_HEADS
    _, page, h, _ = cache_k.shape
    nseq, n_pages = page_table.shape
    per_blk = MOBA_BLOCK // page
    steps = batch * N_HEADS
    n_side = (nseq * n_pages) // steps
    assert MOBA_BLOCK % page == 0 and n_side * steps == nseq * n_pages
    assert n_pages % n_side == 0 and n_side % per_blk == 0
    sps = n_pages // n_side

    def page_spec(t):
        def index(b, hh, pt):
            step = b * N_HEADS + hh
            return (pt[step // sps, (step % sps) * n_side + t], 0, 0, 0)
        return pl.BlockSpec((None, page, h, hd), index)

    spec = pl.BlockSpec((seq, hd), lambda b, hh, pt: (b, hh))
    grid_spec = pltpu.PrefetchScalarGridSpec(
        num_scalar_prefetch=1,
        grid=(batch, N_HEADS),
        in_specs=([pl.BlockSpec(memory_space=pltpu.SMEM), spec, spec, spec]
                  + [page_spec(t) for t in range(n_side)]),
        out_specs=[spec,
                   pl.BlockSpec((None, n_pages // per_blk, h, hd),
                                lambda b, hh, pt: ((b * N_HEADS + hh) // sps, 0, 0, 0))],
        scratch_shapes=[pltpu.VMEM((seq, hd + LANES), BF16)] * 3,
    )
    return pl.pallas_call(
        functools.partial(_moba_prompt_kernel, seq=seq, hd=hd, n_side=n_side,
                          pages_per_blk=per_blk, steps_per_seq=sps),
        grid_spec=grid_spec,
        out_shape=[jax.ShapeDtypeStruct((batch * seq, d), BF16),
                   jax.ShapeDtypeStruct((nseq, n_pages // per_blk, h, hd), F32)],
        compiler_params=_params(2),
        name="moba_prompt",
    )(page_table, slopes, q, k, v, *([cache_k] * n_side))


def _bf16_round(x):
    return x.astype(BF16).astype(F32)


def _select_kernel(q_ref, km_ref, o_ref):
    nblk, h, _ = km_ref.shape
    g = jnp.sum(_bf16_round(km_ref[...]) * _bf16_round(q_ref[...])[None], axis=-1)
    n_idx = lax.broadcasted_iota(jnp.int32, (nblk, h), 0)
    rows = []
    for _ in range(MOBA_TOP_K):
        mx = jnp.max(g, axis=0, keepdims=True)
        first = jnp.min(jnp.where(g == mx, n_idx, nblk), axis=0, keepdims=True)
        rows.append(first)
        g = jnp.where(n_idx == first, -jnp.inf, g)
    rows.append(jnp.zeros((SUBLANES - MOBA_TOP_K, h), jnp.int32))
    o_ref[...] = jnp.concatenate(rows, axis=0)


def _select_blocks(q_s, kmean):
    nseq, nblk, h, hd = kmean.shape
    return pl.pallas_call(
        _select_kernel,
        grid=(nseq,),
        in_specs=[pl.BlockSpec((None, h, hd), lambda s: (s, 0, 0)),
                  pl.BlockSpec((None, nblk, h, hd), lambda s: (s, 0, 0, 0))],
        out_specs=pl.BlockSpec((None, SUBLANES, h), lambda s: (s, 0, 0)),
        out_shape=jax.ShapeDtypeStruct((nseq, SUBLANES, h), jnp.int32),
        compiler_params=_params(1),
        name="select_blocks",
    )(q_s, kmean)


def _moba_sample_kernel(pt_ref, idx_ref, slopes_ref, q_ref, kn_ref, vn_ref, kc_hbm, vc_hbm,
                        o_ref, kd_scr, vd_scr, sem, *, page, past_len, hd):
    per_blk = MOBA_BLOCK // page
    n_pg = MOBA_TOP_K * per_blk
    n_heads = q_ref.shape[0]
    s_id = pl.program_id(0)
    scale = 1.0 / math.sqrt(hd)

    def block_of(h, r):
        return idx_ref[(s_id * n_heads + h) * MOBA_TOP_K + r]

    def head_copies(h):
        out = []
        for t in range(n_pg):
            r, half = divmod(t, per_blk)
            pg = pt_ref[s_id, per_blk * block_of(h, r) + half]
            out.append(pltpu.make_async_copy(kc_hbm.at[pg, :, h, :], kd_scr.at[h, t], sem.at[h]))
            out.append(pltpu.make_async_copy(vc_hbm.at[pg, :, h, :], vd_scr.at[h, t], sem.at[h]))
        return out

    def start_head(h, carry):
        for cp in head_copies(h):
            cp.start()
        return carry

    lax.fori_loop(0, n_heads, start_head, 0)
    row = lax.broadcasted_iota(jnp.int32, (page, 1), 0)

    def attend_head(h, carry):
        for cp in head_copies(h):
            cp.wait()
        slope = slopes_ref[h]
        q = _bf16_round(q_ref[pl.ds(h, 1), :])
        logits = []
        for t in range(n_pg):
            r, half = divmod(t, per_blk)
            s = jnp.sum(_bf16_round(kd_scr[h, t]) * q, axis=1, keepdims=True)
            dist = (past_len - block_of(h, r) * MOBA_BLOCK - half * page - row).astype(F32)
            logits.append(s * scale - slope * dist)
        l_self = jnp.sum(q * _bf16_round(kn_ref[pl.ds(h, 1), :]), axis=1, keepdims=True) * scale
        mx = l_self
        for l in logits:
            mx = jnp.maximum(mx, jnp.max(l, axis=0, keepdims=True))
        p_self = jnp.exp(l_self - mx)
        probs = [jnp.exp(l - mx) for l in logits]
        denom = p_self
        for p in probs:
            denom = denom + jnp.sum(p, axis=0, keepdims=True)
        acc = _bf16_round(p_self / denom) * _bf16_round(vn_ref[pl.ds(h, 1), :])
        for t in range(n_pg):
            acc = acc + jnp.sum(_bf16_round(probs[t] / denom) * _bf16_round(vd_scr[h, t]),
                                axis=0, keepdims=True)
        o_ref[pl.ds(h, 1), :] = acc
        return carry

    lax.fori_loop(0, n_heads, attend_head, 0)


def _moba_sample(page_table, top_idx, slopes, q_s, k_s, v_s, cache_k, cache_v, past_len):
    _, page, h, hd = cache_k.shape
    nseq = q_s.shape[0]
    n_pg = MOBA_TOP_K * (MOBA_BLOCK // page)
    seq_spec = pl.BlockSpec((None, h, hd), lambda s, pt, idx: (s, 0, 0))
    grid_spec = pltpu.PrefetchScalarGridSpec(
        num_scalar_prefetch=2,
        grid=(nseq,),
        in_specs=[pl.BlockSpec(memory_space=pltpu.SMEM), seq_spec, seq_spec, seq_spec,
                  pl.BlockSpec(memory_space=pl.ANY), pl.BlockSpec(memory_space=pl.ANY)],
        out_specs=seq_spec,
        scratch_shapes=[pltpu.VMEM((h, n_pg, page, hd), F32), pltpu.VMEM((h, n_pg, page, hd), F32),
                        pltpu.SemaphoreType.DMA((h,))],
    )
    return pl.pallas_call(
        functools.partial(_moba_sample_kernel, page=page, past_len=past_len, hd=hd),
        grid_spec=grid_spec,
        out_shape=jax.ShapeDtypeStruct((nseq, h, hd), F32),
        compiler_params=_params(1),
        name="moba_sample",
    )(page_table, top_idx, slopes, q_s, k_s, v_s, cache_k, cache_v)


def _sigmoid(x):
    return 0.5 * jnp.tanh(0.5 * x) + 0.5


def _rg_gates(xc, wa_ref, ba_ref, wi_ref, bi_ref, lam_ref):
    xcb = xc.astype(BF16)
    r = _sigmoid(
        jnp.dot(xcb, wa_ref[...].astype(BF16), preferred_element_type=F32) + ba_ref[...])
    i = _sigmoid(
        jnp.dot(xcb, wi_ref[...].astype(BF16), preferred_element_type=F32) + bi_ref[...])
    neg_lam = -lam_ref[...]
    softplus = jnp.maximum(neg_lam, 0.0) + jnp.log1p(jnp.exp(-jnp.abs(neg_lam)))
    log_a = -RG_C * r * softplus
    a = jnp.exp(log_a)
    mult = jnp.sqrt(jnp.tanh(-log_a) * (a * a + 1.0))
    return a, mult * (i * xc)


def _rglru_kernel(xr_ref, yg_ref, xrs_ref, ygs_ref, cs_ref, h0_ref, cw_ref, cb_ref,
                  wa_ref, ba_ref, wi_ref, bi_ref, lam_ref,
                  o_ref, hl_ref, os_ref, hs_ref, xs_scr, a_scr, b_scr, h_scr, *, nb, seq):
    b = pl.program_id(1)
    cw = cw_ref[...]
    width = cw.shape[0]
    gate_refs = (wa_ref, ba_ref, wi_ref, bi_ref, lam_ref)

    @pl.when(b < nb)
    def _():
        x = xr_ref[...]
        xs_scr[0:SUBLANES, :] = jnp.zeros((SUBLANES, x.shape[1]), F32)
        xs_scr[SUBLANES:, :] = x
        xc = xs_scr[pl.ds(SUBLANES - (width - 1), seq), :] * cw[0:1]
        for j in range(1, width):
            xc = xc + xs_scr[pl.ds(SUBLANES - (width - 1) + j, seq), :] * cw[j:j + 1]
        xc = xc + cb_ref[...]
        a, bx = _rg_gates(xc, *gate_refs)
        a_scr[...] = a
        b_scr[...] = bx
        row = lax.broadcasted_iota(jnp.int32, (SUBLANES, x.shape[1]), 0)

        def body(g, h):
            off = pl.multiple_of(g * SUBLANES, SUBLANES)
            aa = a_scr[pl.ds(off, SUBLANES), :]
            bb = b_scr[pl.ds(off, SUBLANES), :]
            for d in (1, 2, 4):
                a_sh = jnp.where(row >= d, pltpu.roll(aa, d, 0), 1.0)
                b_sh = jnp.where(row >= d, pltpu.roll(bb, d, 0), 0.0)
                bb = aa * b_sh + bb
                aa = aa * a_sh
            hh = aa * h + bb
            h_scr[pl.ds(off, SUBLANES), :] = hh
            return hh[SUBLANES - 1:SUBLANES, :]

        h_last = lax.fori_loop(0, seq // SUBLANES, body, jnp.zeros((1, x.shape[1]), F32),
                               unroll=4)
        hl_ref[...] = h_last
        o_ref[...] = (h_scr[...] * yg_ref[...]).astype(o_ref.dtype)

    @pl.when(b == nb)
    def _():
        xc = cs_ref[0] * cw[0:1]
        for j in range(1, width - 1):
            xc = xc + cs_ref[j] * cw[j:j + 1]
        xc = xc + xrs_ref[...] * cw[width - 1:width] + cb_ref[...]
        a, bx = _rg_gates(xc, *gate_refs)
        hh = a * h0_ref[...] + bx
        hs_ref[...] = hh
        os_ref[...] = hh * ygs_ref[...]


def _rglru(xr, yg, xr_s, yg_s, conv_state, h0, conv_w, conv_b, w_a, b_a, w_i, b_i, lam, j,
           batch, seq):
    d = xr.shape[1]
    ms = xr_s.shape[0]
    nblk, cw = w_a.shape[1], w_a.shape[2]
    width = conv_w.shape[1]
    last = batch - 1
    seq_spec = pl.BlockSpec((seq, cw), lambda c, b: (jnp.minimum(b, last), c))
    smp_spec = pl.BlockSpec((ms, cw), lambda c, b: (0, c))
    vec_spec = pl.BlockSpec((None, 1, cw), lambda c, b: (j, 0, c))
    mat_spec = pl.BlockSpec((None, None, cw, cw), lambda c, b: (j, c, 0, 0))
    return pl.pallas_call(
        functools.partial(_rglru_kernel, nb=batch, seq=seq),
        grid=(nblk, batch + 1),
        in_specs=[
            seq_spec, seq_spec, smp_spec, smp_spec,
            pl.BlockSpec((width - 1, ms, cw), lambda c, b: (0, 0, c)),
            pl.BlockSpec((None, ms, cw), lambda c, b: (j, 0, c)),
            pl.BlockSpec((None, width, cw), lambda c, b: (j, 0, c)),
            vec_spec, mat_spec, vec_spec, mat_spec, vec_spec, vec_spec,
        ],
        out_specs=[
            seq_spec,
            pl.BlockSpec((None, 1, cw), lambda c, b: (jnp.minimum(b, last), 0, c)),
            smp_spec, smp_spec,
        ],
        out_shape=[
            jax.ShapeDtypeStruct((batch * seq, d), BF16),
            jax.ShapeDtypeStruct((batch, 1, d), F32),
            jax.ShapeDtypeStruct((ms, d), F32),
            jax.ShapeDtypeStruct((ms, d), F32),
        ],
        scratch_shapes=[pltpu.VMEM((seq + SUBLANES, cw), F32),
                        pltpu.VMEM((seq, cw), F32),
                        pltpu.VMEM((seq, cw), F32),
                        pltpu.VMEM((seq, cw), F32)],
        compiler_params=_params(2),
        name="rglru",
    )(xr, yg, xr_s, yg_s, conv_state, h0, conv_w, conv_b, w_a, b_a, w_i, b_i, lam)


def _gelu_erf(x):
    return 0.5 * x * (1.0 + lax.erf(x * math.sqrt(0.5)))


def _ffn_up_kernel(ap_ref, as_ref, wg_ref, wu_ref, s_ref, cw_ref, cb_ref,
                   hp_ref, hs_ref, tail_ref, gs_ref, wgb_ref, wub_ref, g_scr, *, nm, tiles_per_seq):
    m = pl.program_id(1)
    cw = cw_ref[...]
    width = cw.shape[0]
    tm = ap_ref.shape[0]

    @pl.when(m == 0)
    def _():
        wgb_ref[...] = wg_ref[...].astype(BF16)
        wub_ref[...] = wu_ref[...].astype(BF16)
        g_scr[tm:, :] = jnp.zeros((SUBLANES, g_scr.shape[1]), F32)

    @pl.when(m < nm)
    def _():
        g_scr[0:SUBLANES, :] = jnp.where(m % tiles_per_seq == 0, 0.0, g_scr[tm:, :])
        for r0 in range(0, tm, MM_ROWS):
            a = ap_ref[r0:r0 + MM_ROWS, :]
            g = jnp.dot(a, wgb_ref[...], preferred_element_type=F32)
            g_scr[SUBLANES + r0:SUBLANES + r0 + MM_ROWS, :] = g
            first = SUBLANES + r0 - (width - 1)
            y = g_scr[pl.ds(first, MM_ROWS), :] * cw[0:1]
            for j in range(1, width - 1):
                y = y + g_scr[pl.ds(first + j, MM_ROWS), :] * cw[j:j + 1]
            y = y + g * cw[width - 1:width] + cb_ref[...]
            act = _gelu_erf(y)
            u = jnp.dot(a, wub_ref[...], preferred_element_type=F32)
            hp_ref[r0:r0 + MM_ROWS, :] = (act * u).astype(hp_ref.dtype)
        tail_ref[...] = g_scr[tm:, :]

    @pl.when(m == nm)
    def _():
        a = as_ref[...].astype(BF16)
        g = jnp.dot(a, wgb_ref[...], preferred_element_type=F32)
        u = jnp.dot(a, wub_ref[...], preferred_element_type=F32)
        y = s_ref[0] * cw[0:1]
        for j in range(1, width - 1):
            y = y + s_ref[j] * cw[j:j + 1]
        y = y + g * cw[width - 1:width] + cb_ref[...]
        hs_ref[...] = _gelu_erf(y) * u
        gs_ref[...] = g


def _ffn_up(ap, a_s, w_gate, w_up, state, conv_w, conv_b, layer, batch, seq, tn):
    mp, k = ap.shape
    ms = a_s.shape[0]
    f = w_gate.shape[2]
    width = conv_w.shape[1]
    tm = FFN_ROWS
    assert seq % tm == 0 and tm % MM_ROWS == 0
    nm = mp // tm
    last = nm - 1
    tps = seq // tm
    w_spec = pl.BlockSpec((None, k, tn), lambda n, m: (layer, 0, n))
    return pl.pallas_call(
        functools.partial(_ffn_up_kernel, nm=nm, tiles_per_seq=tps),
        grid=(f // tn, nm + 1),
        in_specs=[
            pl.BlockSpec((tm, k), lambda n, m: (jnp.minimum(m, last), 0)),
            pl.BlockSpec((ms, k), lambda n, m: (0, 0)),
            w_spec, w_spec,
            pl.BlockSpec((width - 1, ms, tn), lambda n, m: (0, 0, n)),
            pl.BlockSpec((None, width, tn), lambda n, m: (layer, 0, n)),
            pl.BlockSpec((None, 1, tn), lambda n, m: (layer, 0, n)),
        ],
        out_specs=[
            pl.BlockSpec((tm, tn), lambda n, m: (jnp.minimum(m, last), n)),
            pl.BlockSpec((ms, tn), lambda n, m: (0, n)),
            pl.BlockSpec((None, SUBLANES, tn), lambda n, m: (jnp.minimum(m, last) // tps, 0, n)),
            pl.BlockSpec((ms, tn), lambda n, m: (0, n)),
        ],
        out_shape=[
            jax.ShapeDtypeStruct((mp, f), BF16),
            jax.ShapeDtypeStruct((ms, f), F32),
            jax.ShapeDtypeStruct((batch, SUBLANES, f), F32),
            jax.ShapeDtypeStruct((ms, f), F32),
        ],
        scratch_shapes=[pltpu.VMEM((k, tn), BF16), pltpu.VMEM((k, tn), BF16),
                        pltpu.VMEM((tm + SUBLANES, tn), F32)],
        compiler_params=_params(2),
        name="ffn_up",
    )(ap, a_s, w_gate, w_up, state, conv_w, conv_b)


def kernel(x_prompt, x_sample, cache_k_l0, cache_v_l0, cache_k_l2, cache_v_l2, page_table, state_rglru_h, state_rglru_conv, state_ffn_conv, norm_mix, norm_ffn, norm_final, attn_w_qkv, attn_w_o, rg_w_gate, rg_w_x, rg_conv_w, rg_conv_b, rg_w_a, rg_b_a, rg_w_i, rg_b_i, rg_lambda, rg_w_out, ffn_w_gate, ffn_w_up, ffn_conv_w, ffn_conv_b, ffn_w_down):
    batch, seq, d = x_prompt.shape
    nseq, dec_seq, _ = x_sample.shape
    assert dec_seq == 1 and seq % MM_ROWS == 0 and seq % MOBA_BLOCK == 0
    depth = norm_mix.shape[0]
    hd = d // N_HEADS
    page = cache_k_l0.shape[1]
    past_len = page_table.shape[1] * page
    assert past_len % MOBA_BLOCK == 0
    d_ff = ffn_w_gate.shape[2]
    mp = batch * seq

    xp = x_prompt.reshape(mp, d)
    xs = x_sample.reshape(nseq, d)
    slopes = jnp.exp2(-8.0 * jnp.arange(1, N_HEADS + 1, dtype=F32) / N_HEADS)
    caches = [(cache_k_l0, cache_v_l0), (cache_k_l2, cache_v_l2)]
    norm_mix3 = norm_mix.reshape(depth, 1, d)
    norm_ffn3 = norm_ffn.reshape(depth, 1, d)
    rg_conv_b3 = rg_conv_b.reshape(-1, 1, d)
    rg_b_a3 = rg_b_a.reshape(-1, 1, d)
    rg_b_i3 = rg_b_i.reshape(-1, 1, d)
    rg_lam3 = rg_lambda.reshape(-1, 1, d)
    ffn_conv_b3 = ffn_conv_b.reshape(depth, 1, d_ff)

    att_out, rg_h_p, rg_h_s, rg_c_p, rg_c_s, ffn_c_p, ffn_c_s = [], [], [], [], [], [], []
    for layer in range(depth):
        j = layer // 2
        hp, hs = _rmsnorm(xp, xs, norm_mix3, layer, BF16)
        if layer % 2 == 0:
            ck, cv = caches[j]
            qp, qs = _matmul(hp, hs, attn_w_qkv, j, col_off=0, n_out=d, tn=1024)
            kp, ks = _matmul(hp, hs, attn_w_qkv, j, col_off=d, n_out=d, tn=1024)
            vp, vs = _matmul(hp, hs, attn_w_qkv, j, col_off=2 * d, n_out=d, tn=1024)
            op, kmean = _moba_prompt(slopes, qp, kp, vp, batch, seq, ck, page_table)
            qs3, ks3, vs3 = (t.reshape(nseq, N_HEADS, hd) for t in (qs, ks, vs))
            picks = _select_blocks(qs3, kmean)
            top_idx = picks[:, :MOBA_TOP_K].transpose(0, 2, 1).reshape(-1)
            o_s = _moba_sample(page_table, top_idx, slopes, qs3, ks3, vs3, ck, cv, past_len)
            xp, xs = _matmul(op, o_s.reshape(nseq, d), attn_w_o, j, n_out=d, tn=1024,
                             res=(xp, xs))
            att_out += [kp.reshape(batch, seq, N_HEADS, hd), vp.reshape(batch, seq, N_HEADS, hd),
                        ks.reshape(nseq, 1, N_HEADS, hd), vs.reshape(nseq, 1, N_HEADS, hd)]
        else:
            ygp, ygs = _matmul(hp, hs, rg_w_gate, j, n_out=d, tn=1024, epilogue="gelu_tanh",
                               out_dtype=BF16)
            xrp, xrs = _matmul(hp, hs, rg_w_x, j, n_out=d, tn=1024)
            conv_state = state_rglru_conv[j].transpose(1, 0, 2)
            gp, hl_p, gs, hl_s = _rglru(xrp, ygp, xrs, ygs, conv_state, state_rglru_h,
                                        rg_conv_w, rg_conv_b3, rg_w_a, rg_b_a3, rg_w_i, rg_b_i3,
                                        rg_lam3, j, batch, seq)
            xp, xs = _matmul(gp, gs, rg_w_out, j, n_out=d, tn=1024, res=(xp, xs))
            keep = rg_conv_w.shape[1] - 1
            rg_h_p.append(hl_p.reshape(batch, d))
            rg_h_s.append(hl_s)
            rg_c_p.append(xrp.reshape(batch, seq, d)[:, seq - keep:])
            rg_c_s.append(jnp.concatenate([state_rglru_conv[j][:, 1:], xrs[:, None, :]], axis=1))
        hp, hs = _rmsnorm(xp, xs, norm_ffn3, layer, BF16)
        f_state = state_ffn_conv[layer].transpose(1, 0, 2)
        up_p, up_s, tail, g_s = _ffn_up(hp, hs, ffn_w_gate, ffn_w_up, f_state, ffn_conv_w,
                                        ffn_conv_b3, layer, batch, seq, tn=512)
        xp, xs = _matmul(up_p, up_s, ffn_w_down, layer, n_out=d, tn=512, res=(xp, xs),
                         w_buffers=1)
        keep = ffn_conv_w.shape[1] - 1
        ffn_c_p.append(tail[:, SUBLANES - keep:])
        ffn_c_s.append(jnp.concatenate([state_ffn_conv[layer][:, 1:], g_s[:, None, :]], axis=1))

    yp, ys = _rmsnorm(xp, xs, norm_final.reshape(1, 1, d), 0, F32)
    return (yp.reshape(batch, seq, d), ys.reshape(nseq, 1, d), *att_out,
            jnp.stack(rg_h_p), jnp.stack(rg_h_s), jnp.stack(rg_c_p), jnp.stack(rg_c_s),
            jnp.stack(ffn_c_p), jnp.stack(ffn_c_s))
```
